```python
import math
import jax, jax.numpy as jnp
from jax import lax
import numpy as np

D_MODEL = 1024
BATCH = 8
SEQ = 4096
DEPTH = 1

MLA_HEADS = 8
MLA_Q_LORA = 256
MLA_KV_LORA = 128
MLA_NOPE = 64
MLA_ROPE = 32
MLA_V = 64
ROPE_THETA = 10000.0
Q_BLOCK = 128
GDN_HEADS = 4
GDN_DK = 128
GDN_DV = 128
GDN_CONV = 5
GDN_CHUNK = 64
D_FF = 4 * D_MODEL
EPS = 1e-6

MLA_WIDTH = MLA_HEADS * MLA_V
GDN_QK = GDN_HEADS * GDN_DK
GDN_WIDTH = GDN_HEADS * GDN_DV
GDN_QKV = 2 * GDN_QK + GDN_WIDTH
IN_SPLITS = (MLA_Q_LORA, MLA_KV_LORA, MLA_ROPE, GDN_QKV,
             GDN_HEADS, GDN_HEADS, GDN_HEADS, GDN_HEADS,
             GDN_WIDTH, D_MODEL, D_MODEL)
D_IN = (MLA_Q_LORA + MLA_KV_LORA + MLA_ROPE + GDN_QKV + 4 * GDN_HEADS
        + GDN_WIDTH + 2 * D_MODEL)

kernel_name = "hybrid_mla_gdn_gated_merge_encoder"


def _rmsnorm(x, w):
    xf = x.astype(jnp.float32)
    y = xf * lax.rsqrt(jnp.mean(xf * xf, axis=-1, keepdims=True) + EPS)
    return (y * w.astype(jnp.float32)).astype(x.dtype)


def _l2norm(x):
    xf = x.astype(jnp.float32)
    return xf * lax.rsqrt(jnp.sum(xf * xf, axis=-1, keepdims=True) + EPS)


def _rope(x, positions):
    d = x.shape[-1]
    inv_freq = 1.0 / (ROPE_THETA ** (jnp.arange(0, d, 2, dtype=jnp.float32) / d))
    ang = positions.astype(jnp.float32)[..., None] * inv_freq
    ang = ang.reshape(ang.shape[:2] + (1,) * (x.ndim - 3) + (d // 2,))
    cos, sin = jnp.cos(ang), jnp.sin(ang)
    xf = x.astype(jnp.float32)
    x1, x2 = xf[..., : d // 2], xf[..., d // 2:]
    out = jnp.concatenate([x1 * cos - x2 * sin, x2 * cos + x1 * sin], axis=-1)
    return out.astype(x.dtype)


def _mla_attention(q_nope, q_rope, k_nope, k_rope, v):
    B, S, H, _ = q_nope.shape
    nb = S // Q_BLOCK
    qn = q_nope.reshape(B, nb, Q_BLOCK, H, MLA_NOPE).swapaxes(0, 1)
    qr = q_rope.reshape(B, nb, Q_BLOCK, H, MLA_ROPE).swapaxes(0, 1)
    scale = 1.0 / math.sqrt(MLA_NOPE + MLA_ROPE)

    def block(args):
        qn_b, qr_b = args
        s = (jnp.einsum('bqhd,bkhd->bhqk', qn_b, k_nope)
             + jnp.einsum('bqhd,bkd->bhqk', qr_b, k_rope))
        p = jax.nn.softmax(s.astype(jnp.float32) * scale, axis=-1).astype(v.dtype)
        return jnp.einsum('bhqk,bkhd->bqhd', p, v)

    o = lax.map(block, (qn, qr))
    return o.swapaxes(0, 1).reshape(B, S, H * MLA_V)


def _to_chunks(t, n, c):
    t = t.reshape((t.shape[0], n, c) + t.shape[2:])
    return jnp.moveaxis(t, 2, 3)


def _gated_delta_chunked(q, k, v, g, beta):
    B, S, H, DK = q.shape
    DV = v.shape[-1]
    C = GDN_CHUNK
    N = S // C
    qc, kc, vc = (_to_chunks(t.astype(jnp.float32), N, C) for t in (q, k, v))
    gc = jnp.cumsum(_to_chunks(g.astype(jnp.float32), N, C), axis=-1)
    bc = _to_chunks(beta.astype(jnp.float32), N, C)

    tril_incl = jnp.tril(jnp.ones((C, C), dtype=bool))
    tril_strict = jnp.tril(jnp.ones((C, C), dtype=bool), -1)
    diff = gc[..., :, None] - gc[..., None, :]
    decay = jnp.where(tril_incl, jnp.exp(jnp.where(tril_incl, diff, 0.0)), 0.0)

    k_beta = kc * bc[..., None]
    v_beta = vc * bc[..., None]
    lower = jnp.where(tril_strict, jnp.einsum('bnhid,bnhjd->bnhij', k_beta, kc) * decay, 0.0)
    a_mat = lower + jnp.eye(C, dtype=jnp.float32)
    rhs = jnp.concatenate([v_beta, k_beta * jnp.exp(gc)[..., None]], axis=-1)
    sol = lax.linalg.triangular_solve(a_mat, rhs, left_side=True, lower=True,
                                      unit_diagonal=True)
    u, w = sol[..., :DV], sol[..., DV:]
    qk = jnp.einsum('bnhid,bnhjd->bnhij', qc, kc) * decay

    def step(state, inp):
        q_i, k_i, u_i, w_i, g_i, qk_i = inp
        v_new = u_i - jnp.einsum('bhcd,bhde->bhce', w_i, state)
        o = (jnp.einsum('bhcd,bhde->bhce', q_i * jnp.exp(g_i)[..., None], state)
             + jnp.einsum('bhij,bhje->bhie', qk_i, v_new))
        g_last = g_i[..., -1]
        k_dec = k_i * jnp.exp(g_last[..., None] - g_i)[..., None]
        state = (state * jnp.exp(g_last)[..., None, None]
                 + jnp.einsum('bhcd,bhce->bhde', k_dec, v_new))
        return state, o

    xs = tuple(jnp.moveaxis(t, 1, 0) for t in (qc, kc, u, w, gc, qk))
    state0 = jnp.zeros((B, H, DK, DV), jnp.float32)
    _, o = lax.scan(step, state0, xs)
    o = jnp.moveaxis(o, 0, 1).swapaxes(2, 3)
    return o.reshape(B, S, H, DV)


def setup_inputs(seed: int = 0) -> dict:
    key = jax.random.key(seed)
    ks = jax.random.split(key, 24)
    f32 = jnp.float32

    def nrm(k, shape, fan_in):
        return jax.random.normal(k, shape, f32) * (fan_in ** -0.5)

    def gain(k, shape):
        return 1.0 + 0.01 * jax.random.normal(k, shape, f32)

    x = jax.random.normal(ks[0], (BATCH, SEQ, D_MODEL), f32)
    offset = jax.random.randint(ks[1], (BATCH, 1), 0, 1024, dtype=jnp.int32)
    positions = (jnp.arange(SEQ, dtype=jnp.int32)[None, :] + offset).astype(jnp.int32)

    a_f = jax.random.uniform(ks[10], (DEPTH, GDN_HEADS), f32, 1.0, 16.0)
    a_b = jax.random.uniform(ks[11], (DEPTH, GDN_HEADS), f32, 1.0, 16.0)
    dt_f = jnp.exp(jax.random.uniform(ks[12], (DEPTH, GDN_HEADS), f32, math.log(1e-3), math.log(1e-1)))
    dt_b = jnp.exp(jax.random.uniform(ks[13], (DEPTH, GDN_HEADS), f32, math.log(1e-3), math.log(1e-1)))

    return {
        "x": x,
        "positions": positions,
        "norm1_w": gain(ks[2], (DEPTH, D_MODEL)),
        "w_in": nrm(ks[3], (DEPTH, D_MODEL, D_IN), D_MODEL),
        "q_norm_w": gain(ks[4], (DEPTH, MLA_Q_LORA)),
        "w_uq": nrm(ks[5], (DEPTH, MLA_Q_LORA, MLA_HEADS * (MLA_NOPE + MLA_ROPE)), MLA_Q_LORA),
        "kv_norm_w": gain(ks[6], (DEPTH, MLA_KV_LORA)),
        "w_ukv": nrm(ks[7], (DEPTH, MLA_KV_LORA, MLA_HEADS * (MLA_NOPE + MLA_V)), MLA_KV_LORA),
        "conv_w": nrm(ks[8], (DEPTH, GDN_CONV, GDN_QKV), GDN_CONV),
        "a_log_f": jnp.log(a_f),
        "dt_bias_f": dt_f + jnp.log(-jnp.expm1(-dt_f)),
        "a_log_b": jnp.log(a_b),
        "dt_bias_b": dt_b + jnp.log(-jnp.expm1(-dt_b)),
        "gdn_norm_w": gain(ks[9], (DEPTH, GDN_DV)),
        "w_proj_a": nrm(ks[14], (DEPTH, MLA_WIDTH, D_MODEL), MLA_WIDTH),
        "w_proj_b": nrm(ks[15], (DEPTH, GDN_WIDTH, D_MODEL), GDN_WIDTH),
        "w_out": nrm(ks[16], (DEPTH, D_MODEL, D_MODEL), D_MODEL),
        "norm2_w": gain(ks[17], (DEPTH, D_MODEL)),
        "w_ff1": nrm(ks[18], (DEPTH, D_MODEL, D_FF), D_MODEL),
        "w_ff2": nrm(ks[19], (DEPTH, D_FF, D_MODEL), D_FF),
        "final_norm_w": gain(ks[20], (D_MODEL,)),
    }


def reference(x, positions, norm1_w, w_in, q_norm_w, w_uq, kv_norm_w, w_ukv, conv_w,
              a_log_f, dt_bias_f, a_log_b, dt_bias_b, gdn_norm_w, w_proj_a, w_proj_b,
              w_out, norm2_w, w_ff1, w_ff2, final_norm_w):
    B, S, _ = x.shape
    split_idx = [int(v) for v in np.cumsum(IN_SPLITS)[:-1]]

    for l in range(DEPTH):
        h = _rmsnorm(x, norm1_w[l])
        z = h @ w_in[l]
        (c_q, c_kv, k_r, gdn_qkv, ga_f, ga_b, gb_f, gb_b,
         gdn_gate, gate_a, gate_b) = jnp.split(z, split_idx, axis=-1)

        q = (_rmsnorm(c_q, q_norm_w[l]) @ w_uq[l]).reshape(B, S, MLA_HEADS, MLA_NOPE + MLA_ROPE)
        q_nope, q_rope = q[..., :MLA_NOPE], _rope(q[..., MLA_NOPE:], positions)
        kv = (_rmsnorm(c_kv, kv_norm_w[l]) @ w_ukv[l]).reshape(B, S, MLA_HEADS, MLA_NOPE + MLA_V)
        k_nope, v_a = kv[..., :MLA_NOPE], kv[..., MLA_NOPE:]
        k_rope = _rope(k_r, positions)
        o_a = _mla_attention(q_nope, q_rope, k_nope, k_rope, v_a)

        cw = conv_w[l].astype(gdn_qkv.dtype)[:, None, :]
        qkv = lax.conv_general_dilated(gdn_qkv, cw, window_strides=(1,),
                                       padding=((GDN_CONV // 2, GDN_CONV // 2),),
                                       dimension_numbers=('NWC', 'WIO', 'NWC'),
                                       feature_group_count=GDN_QKV)
        qkv = jax.nn.silu(qkv)
        gq = _l2norm(qkv[..., :GDN_QK].reshape(B, S, GDN_HEADS, GDN_DK)) * (GDN_DK ** -0.5)
        gk = _l2norm(qkv[..., GDN_QK:2 * GDN_QK].reshape(B, S, GDN_HEADS, GDN_DK))
        gv = qkv[..., 2 * GDN_QK:].reshape(B, S, GDN_HEADS, GDN_DV).astype(jnp.float32)
        g_f = -jnp.exp(a_log_f[l].astype(jnp.float32)) * jax.nn.softplus(
            ga_f.astype(jnp.float32) + dt_bias_f[l].astype(jnp.float32))
        g_b = -jnp.exp(a_log_b[l].astype(jnp.float32)) * jax.nn.softplus(
            ga_b.astype(jnp.float32) + dt_bias_b[l].astype(jnp.float32))
        beta_f = jax.nn.sigmoid(gb_f.astype(jnp.float32))
        beta_b = jax.nn.sigmoid(gb_b.astype(jnp.float32))
        o_fwd = _gated_delta_chunked(gq, gk, gv, g_f, beta_f)
        o_bwd = jnp.flip(_gated_delta_chunked(jnp.flip(gq, 1), jnp.flip(gk, 1), jnp.flip(gv, 1),
                                              jnp.flip(g_b, 1), jnp.flip(beta_b, 1)), 1)
        o_b = _rmsnorm(o_fwd + o_bwd, gdn_norm_w[l]) * jax.nn.silu(
            gdn_gate.reshape(B, S, GDN_HEADS, GDN_DV).astype(jnp.float32))
        o_b = o_b.reshape(B, S, GDN_WIDTH).astype(x.dtype)

        merged = (jax.nn.sigmoid(gate_a) * (o_a @ w_proj_a[l])
                  + jax.nn.sigmoid(gate_b) * (o_b @ w_proj_b[l]))
        x = x + merged @ w_out[l]

        h2 = _rmsnorm(x, norm2_w[l])
        x = x + jnp.square(jax.nn.relu(h2 @ w_ff1[l])) @ w_ff2[l]

    return _rmsnorm(x, final_norm_w)
```

```python
import functools
import math

import jax
import jax.numpy as jnp
from jax import lax
from jax.experimental import pallas as pl
from jax.experimental.pallas import tpu as pltpu

F32 = jnp.float32
BF16 = jnp.bfloat16

D_MODEL = 1024
MLA_HEADS = 8
MLA_Q_LORA = 256
MLA_KV_LORA = 128
MLA_NOPE = 64
MLA_ROPE = 32
MLA_V = 64
ROPE_THETA = 10000.0
GDN_HEADS = 4
GDN_DK = 128
GDN_DV = 128
GDN_CONV = 5
CHUNK = 64
SUB = 16
D_FF = 4 * D_MODEL
EPS = 1e-6

MLA_WIDTH = MLA_HEADS * MLA_V
GDN_QK = GDN_HEADS * GDN_DK
GDN_WIDTH = GDN_HEADS * GDN_DV
GDN_QKV = 2 * GDN_QK + GDN_WIDTH
HEAD_PAD = 128
ROPE_HALF = MLA_ROPE // 2

VMEM_LIMIT_BYTES = 56 * 1024 * 1024


def _dot(a, b):
    return jnp.dot(a, b, preferred_element_type=F32)


def _dot_nt(a, b):
    return lax.dot_general(a, b, (((1,), (1,)), ((), ())), preferred_element_type=F32)


def _dot_tn(a, b):
    return lax.dot_general(a, b, (((0,), (0,)), ((), ())), preferred_element_type=F32)


def _sigmoid(x):
    return 1.0 / (1.0 + jnp.exp(-x))


def _silu(x):
    return x * _sigmoid(x)


def _softplus(x):
    return jnp.maximum(x, 0.0) + jnp.log(1.0 + jnp.exp(-jnp.abs(x)))


def _rms(x, w):
    return x * lax.rsqrt(jnp.mean(x * x, axis=-1, keepdims=True) + EPS) * w


def _chunk_cumsum(x, axis, reverse):
    n = x.shape[axis]
    pos = lax.broadcasted_iota(jnp.int32, x.shape, axis) % CHUNK
    s = 1
    while s < CHUNK:
        if reverse:
            shifted = pltpu.roll(x, n - s, axis)
            x = x + jnp.where(pos < CHUNK - s, shifted, 0.0)
        else:
            shifted = pltpu.roll(x, s, axis)
            x = x + jnp.where(pos >= s, shifted, 0.0)
        s *= 2
    return x


def _decay_beta(z, bias, neg_a, axis):
    idx = lax.broadcasted_iota(jnp.int32, z.shape, 1 - axis)
    g = neg_a * _softplus(z + bias)
    g_f = _chunk_cumsum(jnp.where(idx < GDN_HEADS, g, 0.0), axis, reverse=False)
    g_b = _chunk_cumsum(jnp.where((idx >= GDN_HEADS) & (idx < 2 * GDN_HEADS), g, 0.0), axis, reverse=True)
    return jnp.where(idx < GDN_HEADS, g_f, jnp.where(idx < 2 * GDN_HEADS, g_b, _sigmoid(z)))


def _rope(t, cos, sin_lo, sin_hi):
    return (t * cos + pltpu.roll(t, HEAD_PAD - ROPE_HALF, 1) * sin_lo
            + pltpu.roll(t, ROPE_HALF, 1) * sin_hi)


def _in_proj_kernel(x_ref, pos_ref, n1_ref, wsm_ref, wgdn_ref, wgg_ref, wga_ref, wgb_ref, wgt_ref,
                    qn_ref, wuq_ref, kvn_ref, wuk_ref, wv_ref, ropec_ref, gcc_ref, gcr_ref,
                    q_out, k_out, v_out, zg_out, gg_out, sa_out, sb_out, gcol_out, grow_out):
    hb = _rms(x_ref[...], n1_ref[...]).astype(BF16)

    zs = _dot(hb, wsm_ref[...])
    c_q = zs[:, :MLA_Q_LORA]
    c_kv = zs[:, MLA_Q_LORA:MLA_Q_LORA + MLA_KV_LORA]
    k_r = zs[:, 384:512]
    z_gab = zs[:, 512:640]

    ang = pos_ref[...].astype(F32) * ropec_ref[0:1, :]
    cos = jnp.cos(ang)
    sin = jnp.sin(ang)
    sin_lo = sin * ropec_ref[1:2, :]
    sin_hi = sin * ropec_ref[2:3, :]

    qf = _dot(_rms(c_q, qn_ref[...]).astype(BF16), wuq_ref[...])
    scale = 1.0 / math.sqrt(MLA_NOPE + MLA_ROPE)
    for h in range(MLA_HEADS):
        sl = slice(h * HEAD_PAD, (h + 1) * HEAD_PAD)
        q_out[:, sl] = (_rope(qf[:, sl], cos, sin_lo, sin_hi) * scale).astype(BF16)

    ckv = _rms(c_kv, kvn_ref[...]).astype(BF16)
    kn = _dot(ckv, wuk_ref[...])
    kr = _rope(k_r, cos, sin_lo, sin_hi)
    for h in range(MLA_HEADS):
        sl = slice(h * HEAD_PAD, (h + 1) * HEAD_PAD)
        k_out[:, sl] = (kn[:, sl] + kr).astype(BF16)
    v_out[...] = _dot(ckv, wv_ref[...]).astype(BF16)

    zg_out[...] = _dot(hb, wgdn_ref[...]).astype(BF16)
    gg_out[...] = _silu(_dot(hb, wgg_ref[...])).astype(BF16)
    sa_out[...] = _sigmoid(_dot(hb, wga_ref[...])).astype(BF16)
    sb_out[...] = _sigmoid(_dot(hb, wgb_ref[...])).astype(BF16)

    col = _decay_beta(z_gab, gcc_ref[0:1, :], gcc_ref[1:2, :], axis=0)
    gcol_out[...] = col[:, :4 * GDN_HEADS]
    z_t = _dot_nt(wgt_ref[...], hb)
    grow_out[...] = _decay_beta(z_t, gcr_ref[:, 0:1], gcr_ref[:, 1:2], axis=1)


def _in_proj(x2, pos2, w, tm):
    T = x2.shape[0]
    grid = (T // tm,)

    def full(a):
        return pl.BlockSpec(a.shape, lambda i: (0,) * a.ndim, pipeline_mode=pl.Buffered(1))

    def rows(width):
        return pl.BlockSpec((tm, width), lambda i: (i, 0))

    ins = [x2, pos2, w["n1"], w["w_small"], w["w_gdn"], w["w_gg"], w["w_ga"], w["w_gb"], w["w_gt"],
           w["qn"], w["w_uq"], w["kvn"], w["w_uk"], w["w_v"], w["rope_c"], w["g_col_c"], w["g_row_c"]]
    in_specs = [rows(D_MODEL), rows(1)] + [full(a) for a in ins[2:]]
    out_shape = [
        jax.ShapeDtypeStruct((T, MLA_HEADS * HEAD_PAD), BF16),
        jax.ShapeDtypeStruct((T, MLA_HEADS * HEAD_PAD), BF16),
        jax.ShapeDtypeStruct((T, MLA_WIDTH), BF16),
        jax.ShapeDtypeStruct((T, GDN_QKV), BF16),
        jax.ShapeDtypeStruct((T, GDN_WIDTH), BF16),
        jax.ShapeDtypeStruct((T, D_MODEL), BF16),
        jax.ShapeDtypeStruct((T, D_MODEL), BF16),
        jax.ShapeDtypeStruct((T, 4 * GDN_HEADS), F32),
        jax.ShapeDtypeStruct((4 * GDN_HEADS, T), F32),
    ]
    out_specs = [rows(MLA_HEADS * HEAD_PAD), rows(MLA_HEADS * HEAD_PAD), rows(MLA_WIDTH), rows(GDN_QKV),
                 rows(GDN_WIDTH), rows(D_MODEL), rows(D_MODEL), rows(4 * GDN_HEADS),
                 pl.BlockSpec((4 * GDN_HEADS, tm), lambda i: (0, i))]
    return pl.pallas_call(
        _in_proj_kernel, grid=grid, in_specs=in_specs, out_specs=out_specs, out_shape=out_shape,
        name="in_proj",
        compiler_params=pltpu.CompilerParams(dimension_semantics=("parallel",),
                                             vmem_limit_bytes=VMEM_LIMIT_BYTES),
    )(*ins)


def _attn_kernel(q_ref, k_ref, v_ref, o_ref):
    for h in range(MLA_HEADS):
        q = q_ref[:, h * HEAD_PAD:(h + 1) * HEAD_PAD]
        k = k_ref[:, h * HEAD_PAD:(h + 1) * HEAD_PAD]
        s = _dot_nt(q, k)
        p = jnp.exp(s - jnp.max(s, axis=-1, keepdims=True))
        l = jnp.sum(p, axis=-1, keepdims=True)
        o = _dot(p.astype(BF16), v_ref[:, h * MLA_V:(h + 1) * MLA_V])
        o_ref[:, h * MLA_V:(h + 1) * MLA_V] = (o / l).astype(BF16)


def _attention(q, k, v, tq):
    B, S, _ = q.shape
    return pl.pallas_call(
        _attn_kernel, grid=(B, S // tq),
        in_specs=[pl.BlockSpec((None, tq, MLA_HEADS * HEAD_PAD), lambda b, i: (b, i, 0)),
                  pl.BlockSpec((None, S, MLA_HEADS * HEAD_PAD), lambda b, i: (b, 0, 0)),
                  pl.BlockSpec((None, S, MLA_WIDTH), lambda b, i: (b, 0, 0))],
        out_specs=pl.BlockSpec((None, tq, MLA_WIDTH), lambda b, i: (b, i, 0)),
        out_shape=jax.ShapeDtypeStruct((B, S, MLA_WIDTH), BF16),
        name="mla_attention",
        compiler_params=pltpu.CompilerParams(dimension_semantics=("parallel", "parallel"),
                                             vmem_limit_bytes=VMEM_LIMIT_BYTES),
    )(q, k, v)


def _unit_tri_solve(l_mat, rhs, row, col):
    on_diag = (row // SUB) == (col // SUB)
    eye = (row == col).astype(F32)
    n1 = jnp.where(on_diag, -l_mat, 0.0)
    l_off = jnp.where(on_diag, 0.0, l_mat)
    n2 = _dot(n1, n1)
    d = eye + n1
    d = d + _dot(d, n2)
    n4 = _dot(n2, n2)
    d = d + _dot(d, n4)
    n8 = _dot(n4, n4)
    d = d + _dot(d, n8)
    dr = _dot(d, jnp.concatenate([l_off, rhs], axis=1))
    m = dr[:, :CHUNK]
    y0 = dr[:, CHUNK:]
    y = y0 - _dot(m, y0)
    y = y0 - _dot(m, y)
    return y0 - _dot(m, y)


def _gdn_kernel(zq_ref, zk_ref, zv_ref, cwq_ref, cwk_ref, cwv_ref, gcol_ref, grow_ref, gate_ref, nw_ref,
                o_ref,
                pad_s, q_s, k_s, v_s, uf_s, ub_s, wf_s, wb_s, qef_s, qeb_s, kdf_s, kdb_s,
                qkf_s, qkb_s, of_s, ob_s):
    S = zq_ref.shape[0]
    n_chunks = S // CHUNK
    halo = 8
    rb = min(512, S)

    pad_s[0:halo, :] = jnp.zeros((halo, GDN_DK), F32)
    pad_s[halo + S:halo + S + halo, :] = jnp.zeros((halo, GDN_DK), F32)

    def conv_stream(z_ref, cw_ref, dst, l2, mult):
        pad_s[halo:halo + S, :] = z_ref[...].astype(F32)
        for r0 in range(0, S, rb):
            acc = jnp.zeros((rb, GDN_DK), F32)
            for j in range(GDN_CONV):
                off = halo + r0 + j - GDN_CONV // 2
                acc = acc + pad_s[off:off + rb, :] * cw_ref[j:j + 1, :]
            y = _silu(acc)
            if l2:
                y = y * (lax.rsqrt(jnp.sum(y * y, axis=-1, keepdims=True) + EPS) * mult)
            dst[r0:r0 + rb, :] = y

    conv_stream(zq_ref, cwq_ref, q_s, True, GDN_DK ** -0.5)
    conv_stream(zk_ref, cwk_ref, k_s, True, 1.0)
    conv_stream(zv_ref, cwv_ref, v_s, False, 1.0)

    row = lax.broadcasted_iota(jnp.int32, (CHUNK, CHUNK), 0)
    col = lax.broadcasted_iota(jnp.int32, (CHUNK, CHUNK), 1)

    def prep(c, carry):
        r0 = pl.multiple_of(c * CHUNK, CHUNK)
        rows = pl.ds(r0, CHUNK)
        qc = q_s[rows, :]
        kc = k_s[rows, :]
        vc = v_s[rows, :]
        kb = kc.astype(BF16)
        gram = _dot_nt(kb, kb)
        qk = _dot_nt(qc.astype(BF16), kb)
        gc = gcol_ref[rows, :]
        gr = grow_ref[c]

        def one_direction(g_col, g_row, beta, keep_incl, keep_strict, g_total, u_s, w_s, qe_s, kd_s, qk_s):
            diff = g_col - g_row
            decay = jnp.where(keep_incl, jnp.exp(jnp.where(keep_incl, diff, 0.0)), 0.0)
            l_mat = jnp.where(keep_strict, gram * beta * decay, 0.0)
            e_g = jnp.exp(g_col)
            rhs = jnp.concatenate([vc * beta, kc * (beta * e_g)], axis=1)
            sol = _unit_tri_solve(l_mat, rhs, row, col)
            u_s[rows, :] = sol[:, :GDN_DV]
            w_s[rows, :] = sol[:, GDN_DV:].astype(BF16)
            qe_s[rows, :] = (qc * e_g).astype(BF16)
            kd_s[rows, :] = (kc * jnp.exp(g_total - g_col)).astype(BF16)
            qk_s[rows, :] = (qk * decay).astype(BF16)

        one_direction(gc[:, 0:1], gr[0:1, :], gc[:, 2:3], row >= col, row > col,
                      gc[CHUNK - 1:CHUNK, 0:1], uf_s, wf_s, qef_s, kdf_s, qkf_s)
        one_direction(gc[:, 1:2], gr[1:2, :], gc[:, 3:4], row <= col, row < col,
                      gc[0:1, 1:2], ub_s, wb_s, qeb_s, kdb_s, qkb_s)
        return carry

    lax.fori_loop(0, n_chunks, prep, 0)

    def scan_step(c, state, g_total, u_s, w_s, qe_s, kd_s, qk_s, out_s):
        rows = pl.ds(pl.multiple_of(c * CHUNK, CHUNK), CHUNK)
        sb = state.astype(BF16)
        v_new = u_s[rows, :] - _dot(w_s[rows, :], sb)
        vb = v_new.astype(BF16)
        out_s[rows, :] = _dot(qe_s[rows, :], sb) + _dot(qk_s[rows, :], vb)
        return state * jnp.exp(g_total) + _dot_tn(kd_s[rows, :], vb)

    def scan(t, carry):
        st_f, st_b = carry
        cf = t
        cb = n_chunks - 1 - t
        gtot_f = gcol_ref[pl.ds(cf * CHUNK + CHUNK - 1, 1), 0:1]
        gtot_b = gcol_ref[pl.ds(cb * CHUNK, 1), 1:2]
        st_f = scan_step(cf, st_f, gtot_f, uf_s, wf_s, qef_s, kdf_s, qkf_s, of_s)
        st_b = scan_step(cb, st_b, gtot_b, ub_s, wb_s, qeb_s, kdb_s, qkb_s, ob_s)
        return st_f, st_b

    zero = jnp.zeros((GDN_DK, GDN_DV), F32)
    lax.fori_loop(0, n_chunks, scan, (zero, zero))

    for r0 in range(0, S, rb):
        o = of_s[r0:r0 + rb, :] + ob_s[r0:r0 + rb, :]
        o_ref[r0:r0 + rb, :] = (_rms(o, nw_ref[...]) * gate_ref[r0:r0 + rb, :].astype(F32)).astype(BF16)


def _gdn(zg, conv_w, gcol, grow, gate, norm_w):
    B, S, _ = zg.shape
    H = GDN_HEADS
    n_chunks = S // CHUNK

    def stream(k):
        return pl.BlockSpec((None, S, GDN_DK), lambda b, h, k=k: (b, 0, k * H + h))

    def cw(k):
        return pl.BlockSpec((GDN_CONV, GDN_DK), lambda b, h, k=k: (0, k * H + h))

    f32_rows = pltpu.VMEM((S, GDN_DK), F32)
    bf_rows = pltpu.VMEM((S, GDN_DK), BF16)
    scratch = [pltpu.VMEM((S + 16, GDN_DK), F32)] + [f32_rows] * 5 + [bf_rows] * 6 + \
              [pltpu.VMEM((S, CHUNK), BF16)] * 2 + [f32_rows] * 2
    return pl.pallas_call(
        _gdn_kernel, grid=(B, H),
        in_specs=[stream(0), stream(1), stream(2), cw(0), cw(1), cw(2),
                  pl.BlockSpec((None, None, S, 4), lambda b, h: (b, h, 0, 0)),
                  pl.BlockSpec((None, None, n_chunks, 4, CHUNK), lambda b, h: (b, h, 0, 0, 0)),
                  pl.BlockSpec((None, S, GDN_DV), lambda b, h: (b, 0, h)),
                  pl.BlockSpec((1, GDN_DV), lambda b, h: (0, 0))],
        out_specs=pl.BlockSpec((None, S, GDN_DV), lambda b, h: (b, 0, h)),
        out_shape=jax.ShapeDtypeStruct((B, S, GDN_WIDTH), BF16),
        scratch_shapes=scratch,
        name="gated_deltanet",
        compiler_params=pltpu.CompilerParams(dimension_semantics=("parallel", "parallel"),
                                             vmem_limit_bytes=VMEM_LIMIT_BYTES),
    )(zg, zg, zg, conv_w, conv_w, conv_w, gcol, grow, gate, norm_w)


def _out_kernel(x_ref, oa_ref, ob_ref, sa_ref, sb_ref, wpa_ref, wpb_ref, wo_ref, n2_ref, w1_ref, w2_ref,
                nf_ref, y_ref):
    merged = (sa_ref[...].astype(F32) * _dot(oa_ref[...], wpa_ref[...])
              + sb_ref[...].astype(F32) * _dot(ob_ref[...], wpb_ref[...]))
    x1 = x_ref[...] + _dot(merged.astype(BF16), wo_ref[...])
    h2 = _rms(x1, n2_ref[...]).astype(BF16)
    a = jnp.maximum(_dot(h2, w1_ref[...]), 0.0)
    x2 = x1 + _dot((a * a).astype(BF16), w2_ref[...])
    y_ref[...] = _rms(x2, nf_ref[...])


def _out_proj(x2, oa, ob, sa, sb, w, tm):
    T = x2.shape[0]

    def full(a):
        return pl.BlockSpec(a.shape, lambda i: (0,) * a.ndim, pipeline_mode=pl.Buffered(1))

    def rows(width):
        return pl.BlockSpec((tm, width), lambda i: (i, 0))

    consts = [w["w_pa"], w["w_pb"], w["w_o"], w["n2"], w["w_1"], w["w_2"], w["nf"]]
    return pl.pallas_call(
        _out_kernel, grid=(T // tm,),
        in_specs=[rows(D_MODEL), rows(MLA_WIDTH), rows(GDN_WIDTH), rows(D_MODEL), rows(D_MODEL)]
                 + [full(a) for a in consts],
        out_specs=rows(D_MODEL),
        out_shape=jax.ShapeDtypeStruct((T, D_MODEL), F32),
        name="merge_mlp",
        compiler_params=pltpu.CompilerParams(dimension_semantics=("parallel",),
                                             vmem_limit_bytes=VMEM_LIMIT_BYTES),
    )(x2, oa, ob, sa, sb, *consts)


def _prepare_weights(norm1_w, w_in, q_norm_w, w_uq, kv_norm_w, w_ukv, a_log_f, dt_bias_f, a_log_b,
                     dt_bias_b, w_proj_a, w_proj_b, w_out, norm2_w, w_ff1, w_ff2, final_norm_w):
    H = GDN_HEADS
    o_kv = MLA_Q_LORA
    o_kr = o_kv + MLA_KV_LORA
    o_gdn = o_kr + MLA_ROPE
    o_gab = o_gdn + GDN_QKV
    o_gg = o_gab + 4 * H
    o_ga = o_gg + GDN_WIDTH
    o_gb = o_ga + D_MODEL
    zeros = lambda n: jnp.zeros((D_MODEL, n), F32)
    w_kr = jnp.concatenate([zeros(MLA_NOPE), w_in[:, o_kr:o_gdn], zeros(HEAD_PAD - MLA_NOPE - MLA_ROPE)], 1)
    w_gab = w_in[:, o_gab:o_gg]
    w_small = jnp.concatenate([w_in[:, :o_kr], w_kr, w_gab, zeros(HEAD_PAD - 4 * H)], 1)

    dq = MLA_NOPE + MLA_ROPE
    w_uq_p = jnp.pad(w_uq.reshape(MLA_Q_LORA, MLA_HEADS, dq), ((0, 0), (0, 0), (0, HEAD_PAD - dq)))
    w_ukv_r = w_ukv.reshape(MLA_KV_LORA, MLA_HEADS, MLA_NOPE + MLA_V)
    w_uk_p = jnp.pad(w_ukv_r[:, :, :MLA_NOPE], ((0, 0), (0, 0), (0, HEAD_PAD - MLA_NOPE)))
    w_v = w_ukv_r[:, :, MLA_NOPE:]

    inv_freq = 1.0 / (ROPE_THETA ** (jnp.arange(0, MLA_ROPE, 2, dtype=F32) / MLA_ROPE))
    z16 = jnp.zeros((ROPE_HALF,), F32)
    one16 = jnp.ones((ROPE_HALF,), F32)
    lane = lambda lo, hi: jnp.concatenate([jnp.zeros((MLA_NOPE,), F32), lo, hi,
                                           jnp.zeros((HEAD_PAD - MLA_NOPE - MLA_ROPE,), F32)])
    rope_c = jnp.stack([lane(inv_freq, inv_freq), lane(-one16, z16), lane(z16, one16)])

    bias = jnp.concatenate([dt_bias_f, dt_bias_b, jnp.zeros((2 * H,), F32)])
    neg_a = jnp.concatenate([-jnp.exp(a_log_f), -jnp.exp(a_log_b), jnp.zeros((2 * H,), F32)])
    g_row_c = jnp.stack([bias, neg_a], axis=1)
    g_col_c = jnp.pad(jnp.stack([bias, neg_a]), ((0, 0), (0, HEAD_PAD - 4 * H)))

    return {
        "n1": norm1_w.reshape(1, D_MODEL),
        "w_small": w_small.astype(BF16),
        "w_gdn": w_in[:, o_gdn:o_gab].astype(BF16),
        "w_gg": w_in[:, o_gg:o_ga].astype(BF16),
        "w_ga": w_in[:, o_ga:o_gb].astype(BF16),
        "w_gb": w_in[:, o_gb:].astype(BF16),
        "w_gt": w_gab.T.astype(BF16),
        "qn": q_norm_w.reshape(1, MLA_Q_LORA),
        "w_uq": w_uq_p.reshape(MLA_Q_LORA, MLA_HEADS * HEAD_PAD).astype(BF16),
        "kvn": kv_norm_w.reshape(1, MLA_KV_LORA),
        "w_uk": w_uk_p.reshape(MLA_KV_LORA, MLA_HEADS * HEAD_PAD).astype(BF16),
        "w_v": w_v.reshape(MLA_KV_LORA, MLA_WIDTH).astype(BF16),
        "rope_c": rope_c,
        "g_col_c": g_col_c,
        "g_row_c": g_row_c,
        "w_pa": w_proj_a.astype(BF16),
        "w_pb": w_proj_b.astype(BF16),
        "w_o": w_out.astype(BF16),
        "n2": norm2_w.reshape(1, D_MODEL),
        "w_1": w_ff1.astype(BF16),
        "w_2": w_ff2.astype(BF16),
        "nf": final_norm_w.reshape(1, D_MODEL),
    }


def _layer(x, positions, conv_w, gdn_norm_w, w):
    B, S, _ = x.shape
    T = B * S
    H = GDN_HEADS
    tm = min(512, S)
    x2 = x.reshape(T, D_MODEL)
    q, k, v, zg, gg, sa, sb, gcol, grow = _in_proj(x2, positions.reshape(T, 1), w, tm)

    o_a = _attention(q.reshape(B, S, -1), k.reshape(B, S, -1), v.reshape(B, S, -1), min(256, S))

    gcol_h = gcol.reshape(B, S, 4, H).transpose(0, 3, 1, 2)
    grow_h = grow.reshape(4, H, B, S // CHUNK, CHUNK).transpose(2, 1, 3, 0, 4)
    o_b = _gdn(zg.reshape(B, S, GDN_QKV), conv_w, gcol_h, grow_h, gg.reshape(B, S, GDN_WIDTH),
               gdn_norm_w.reshape(1, GDN_DV))

    y = _out_proj(x2, o_a.reshape(T, MLA_WIDTH), o_b.reshape(T, GDN_WIDTH), sa, sb, w, tm)
    return y.reshape(B, S, D_MODEL)


def kernel(x, positions, norm1_w, w_in, q_norm_w, w_uq, kv_norm_w, w_ukv, conv_w, a_log_f, dt_bias_f,
           a_log_b, dt_bias_b, gdn_norm_w, w_proj_a, w_proj_b, w_out, norm2_w, w_ff1, w_ff2,
           final_norm_w):
    assert norm1_w.shape[0] == 1, "single-layer block"
    w = _prepare_weights(norm1_w[0], w_in[0], q_norm_w[0], w_uq[0], kv_norm_w[0], w_ukv[0], a_log_f[0],
                         dt_bias_f[0], a_log_b[0], dt_bias_b[0], w_proj_a[0], w_proj_b[0], w_out[0],
                         norm2_w[0], w_ff1[0], w_ff2[0], final_norm_w)
    return _layer(x, positions, conv_w[0], gdn_norm_w[0], w)
```

```python
import functools
import math

import jax
import jax.numpy as jnp
from jax import lax
from jax.experimental import pallas as pl
from jax.experimental.pallas import tpu as pltpu

F32 = jnp.float32
BF16 = jnp.bfloat16

D_MODEL = 1024
MLA_HEADS = 8
MLA_Q_LORA = 256
MLA_KV_LORA = 128
MLA_NOPE = 64
MLA_ROPE = 32
MLA_V = 64
ROPE_THETA = 10000.0
GDN_HEADS = 4
GDN_DK = 128
GDN_DV = 128
GDN_CONV = 5
CHUNK = 64
SUB = 16
D_FF = 4 * D_MODEL
EPS = 1e-6

MLA_WIDTH = MLA_HEADS * MLA_V
GDN_QK = GDN_HEADS * GDN_DK
GDN_WIDTH = GDN_HEADS * GDN_DV
GDN_QKV = 2 * GDN_QK + GDN_WIDTH
HEAD_PAD = 128
ROPE_HALF = MLA_ROPE // 2

VMEM_LIMIT_BYTES = 56 * 1024 * 1024


def _dot(a, b):
    return jnp.dot(a, b, preferred_element_type=F32)


def _dot_nt(a, b):
    return lax.dot_general(a, b, (((1,), (1,)), ((), ())), preferred_element_type=F32)


def _dot_tn(a, b):
    return lax.dot_general(a, b, (((0,), (0,)), ((), ())), preferred_element_type=F32)


def _sigmoid(x):
    return 1.0 / (1.0 + jnp.exp(-x))


def _silu(x):
    return x * _sigmoid(x)


def _softplus(x):
    return jnp.maximum(x, 0.0) + jnp.log(1.0 + jnp.exp(-jnp.abs(x)))


def _rms(x, w):
    return x * lax.rsqrt(jnp.mean(x * x, axis=-1, keepdims=True) + EPS) * w


def _chunk_cumsum(x, axis, reverse):
    n = x.shape[axis]
    pos = lax.broadcasted_iota(jnp.int32, x.shape, axis) % CHUNK
    s = 1
    while s < CHUNK:
        if reverse:
            shifted = pltpu.roll(x, n - s, axis)
            x = x + jnp.where(pos < CHUNK - s, shifted, 0.0)
        else:
            shifted = pltpu.roll(x, s, axis)
            x = x + jnp.where(pos >= s, shifted, 0.0)
        s *= 2
    return x


def _decay_beta(z, bias, neg_a, axis):
    idx = lax.broadcasted_iota(jnp.int32, z.shape, 1 - axis)
    g = neg_a * _softplus(z + bias)
    g_f = _chunk_cumsum(jnp.where(idx < GDN_HEADS, g, 0.0), axis, reverse=False)
    g_b = _chunk_cumsum(jnp.where((idx >= GDN_HEADS) & (idx < 2 * GDN_HEADS), g, 0.0), axis, reverse=True)
    return jnp.where(idx < GDN_HEADS, g_f, jnp.where(idx < 2 * GDN_HEADS, g_b, _sigmoid(z)))


def _rope(t, cos, sin_lo, sin_hi):
    return (t * cos + pltpu.roll(t, HEAD_PAD - ROPE_HALF, 1) * sin_lo
            + pltpu.roll(t, ROPE_HALF, 1) * sin_hi)


def _in_proj_kernel(x_ref, pos_ref, n1_ref, wsm_ref, wgdn_ref, wgg_ref, wga_ref, wgb_ref, wgt_ref,
                    qn_ref, wuq_ref, kvn_ref, wuk_ref, wv_ref, ropec_ref, gcc_ref, gcr_ref,
                    q_out, k_out, v_out, zg_out, gg_out, sa_out, sb_out, gcol_out, grow_out):
    hb = _rms(x_ref[...], n1_ref[...]).astype(BF16)

    zs = _dot(hb, wsm_ref[...])
    c_q = zs[:, :MLA_Q_LORA]
    c_kv = zs[:, MLA_Q_LORA:MLA_Q_LORA + MLA_KV_LORA]
    k_r = zs[:, 384:512]
    z_gab = zs[:, 512:640]

    ang = pos_ref[...].astype(F32) * ropec_ref[0:1, :]
    cos = jnp.cos(ang)
    sin = jnp.sin(ang)
    sin_lo = sin * ropec_ref[1:2, :]
    sin_hi = sin * ropec_ref[2:3, :]

    qf = _dot(_rms(c_q, qn_ref[...]).astype(BF16), wuq_ref[...])
    scale = 1.0 / math.sqrt(MLA_NOPE + MLA_ROPE)
    for h in range(MLA_HEADS):
        sl = slice(h * HEAD_PAD, (h + 1) * HEAD_PAD)
        q_out[:, sl] = (_rope(qf[:, sl], cos, sin_lo, sin_hi) * scale).astype(BF16)

    ckv = _rms(c_kv, kvn_ref[...]).astype(BF16)
    kn = _dot(ckv, wuk_ref[...])
    kr = _rope(k_r, cos, sin_lo, sin_hi)
    for h in range(MLA_HEADS):
        sl = slice(h * HEAD_PAD, (h + 1) * HEAD_PAD)
        k_out[:, sl] = (kn[:, sl] + kr).astype(BF16)
    v_out[...] = _dot(ckv, wv_ref[...]).astype(BF16)

    zg_out[...] = _dot(hb, wgdn_ref[...]).astype(BF16)
    gg_out[...] = _silu(_dot(hb, wgg_ref[...])).astype(BF16)
    sa_out[...] = _sigmoid(_dot(hb, wga_ref[...])).astype(BF16)
    sb_out[...] = _sigmoid(_dot(hb, wgb_ref[...])).astype(BF16)

    col = _decay_beta(z_gab, gcc_ref[0:1, :], gcc_ref[1:2, :], axis=0)
    gcol_out[...] = col[:, :4 * GDN_HEADS]
    z_t = _dot_nt(wgt_ref[...], hb)
    grow_out[...] = _decay_beta(z_t, gcr_ref[:, 0:1], gcr_ref[:, 1:2], axis=1)


def _in_proj(x2, pos2, w, tm):
    T = x2.shape[0]
    grid = (T // tm,)

    def full(a):
        return pl.BlockSpec(a.shape, lambda i: (0,) * a.ndim, pipeline_mode=pl.Buffered(1))

    def rows(width):
        return pl.BlockSpec((tm, width), lambda i: (i, 0))

    ins = [x2, pos2, w["n1"], w["w_small"], w["w_gdn"], w["w_gg"], w["w_ga"], w["w_gb"], w["w_gt"],
           w["qn"], w["w_uq"], w["kvn"], w["w_uk"], w["w_v"], w["rope_c"], w["g_col_c"], w["g_row_c"]]
    in_specs = [rows(D_MODEL), rows(1)] + [full(a) for a in ins[2:]]
    out_shape = [
        jax.ShapeDtypeStruct((T, MLA_HEADS * HEAD_PAD), BF16),
        jax.ShapeDtypeStruct((T, MLA_HEADS * HEAD_PAD), BF16),
        jax.ShapeDtypeStruct((T, MLA_WIDTH), BF16),
        jax.ShapeDtypeStruct((T, GDN_QKV), BF16),
        jax.ShapeDtypeStruct((T, GDN_WIDTH), BF16),
        jax.ShapeDtypeStruct((T, D_MODEL), BF16),
        jax.ShapeDtypeStruct((T, D_MODEL), BF16),
        jax.ShapeDtypeStruct((T, 4 * GDN_HEADS), F32),
        jax.ShapeDtypeStruct((4 * GDN_HEADS, T), F32),
    ]
    out_specs = [rows(MLA_HEADS * HEAD_PAD), rows(MLA_HEADS * HEAD_PAD), rows(MLA_WIDTH), rows(GDN_QKV),
                 rows(GDN_WIDTH), rows(D_MODEL), rows(D_MODEL), rows(4 * GDN_HEADS),
                 pl.BlockSpec((4 * GDN_HEADS, tm), lambda i: (0, i))]
    return pl.pallas_call(
        _in_proj_kernel, grid=grid, in_specs=in_specs, out_specs=out_specs, out_shape=out_shape,
        name="in_proj",
        compiler_params=pltpu.CompilerParams(dimension_semantics=("parallel",),
                                             vmem_limit_bytes=VMEM_LIMIT_BYTES),
    )(*ins)


def _attn_kernel(q_ref, k_ref, v_ref, o_ref):
    for h in range(MLA_HEADS):
        q = q_ref[:, h * HEAD_PAD:(h + 1) * HEAD_PAD]
        k = k_ref[:, h * HEAD_PAD:(h + 1) * HEAD_PAD]
        s = _dot_nt(q, k)
        p = jnp.exp(s - jnp.max(s, axis=-1, keepdims=True))
        l = jnp.sum(p, axis=-1, keepdims=True)
        o = _dot(p.astype(BF16), v_ref[:, h * MLA_V:(h + 1) * MLA_V])
        o_ref[:, h * MLA_V:(h + 1) * MLA_V] = (o / l).astype(BF16)


def _attention(q, k, v, tq):
    B, S, _ = q.shape
    return pl.pallas_call(
        _attn_kernel, grid=(B, S // tq),
        in_specs=[pl.BlockSpec((None, tq, MLA_HEADS * HEAD_PAD), lambda b, i: (b, i, 0)),
                  pl.BlockSpec((None, S, MLA_HEADS * HEAD_PAD), lambda b, i: (b, 0, 0)),
                  pl.BlockSpec((None, S, MLA_WIDTH), lambda b, i: (b, 0, 0))],
        out_specs=pl.BlockSpec((None, tq, MLA_WIDTH), lambda b, i: (b, i, 0)),
        out_shape=jax.ShapeDtypeStruct((B, S, MLA_WIDTH), BF16),
        name="mla_attention",
        compiler_params=pltpu.CompilerParams(dimension_semantics=("parallel", "parallel"),
                                             vmem_limit_bytes=VMEM_LIMIT_BYTES),
    )(q, k, v)


def _bmm(a, b):
    return jnp.einsum("kij,kjl->kil", a, b, preferred_element_type=F32)


def _bmm_nt(a, b):
    return jnp.einsum("kid,kjd->kij", a, b, preferred_element_type=F32)


def _unit_tri_solve(l_mat, rhs, on_diag, eye):
    n1 = jnp.where(on_diag, -l_mat, 0.0)
    l_off = jnp.where(on_diag, 0.0, l_mat)
    n2 = _bmm(n1, n1)
    d = eye + n1
    d = d + _bmm(d, n2)
    n4 = _bmm(n2, n2)
    d = d + _bmm(d, n4)
    n8 = _bmm(n4, n4)
    d = d + _bmm(d, n8)
    m = _bmm(d, l_off)
    y0 = _bmm(d, rhs)
    m2 = _bmm(m, m)
    z1 = y0 - _bmm(m, y0)
    return z1 + _bmm(m2, z1)


GDN_HALO = 16


def _gdn_prep_kernel(z_ref, zp_ref, zn_ref, cw_ref, gcol_ref, grow_ref,
                     uf_ref, wf_ref, qef_ref, kdf_ref, qkf_ref,
                     ub_ref, wb_ref, qeb_ref, kdb_ref, qkb_ref, et_ref, pad_s):
    seg = z_ref.shape[0]
    K = seg // CHUNK
    j = pl.program_id(1)
    halo = GDN_HALO

    pad_s[0:halo, :] = jnp.where(j == 0, 0.0, zp_ref[...].astype(F32))
    pad_s[halo:halo + seg, :] = z_ref[...].astype(F32)
    pad_s[halo + seg:, :] = jnp.where(j == pl.num_programs(1) - 1, 0.0, zn_ref[...].astype(F32))

    def conv(c0, l2, mult):
        acc = jnp.zeros((seg, GDN_DK), F32)
        for t in range(GDN_CONV):
            off = halo + t - GDN_CONV // 2
            acc = acc + pad_s[off:off + seg, c0:c0 + GDN_DK] * cw_ref[t:t + 1, c0:c0 + GDN_DK]
        y = _silu(acc)
        if l2:
            y = y * (lax.rsqrt(jnp.sum(y * y, axis=-1, keepdims=True) + EPS) * mult)
        return y.reshape(K, CHUNK, GDN_DK)

    row = lax.broadcasted_iota(jnp.int32, (CHUNK, CHUNK), 0)
    col = lax.broadcasted_iota(jnp.int32, (CHUNK, CHUNK), 1)
    on_diag = (row // SUB) == (col // SUB)
    eye = (row == col).astype(F32)
    gc = gcol_ref[...].reshape(K, CHUNK, 4 * GDN_HEADS)

    for h in range(GDN_HEADS):
        hs = slice(h * GDN_DK, (h + 1) * GDN_DK)
        q3 = conv(h * GDN_DK, True, GDN_DK ** -0.5)
        k3 = conv(GDN_QK + h * GDN_DK, True, 1.0)
        v3 = conv(2 * GDN_QK + h * GDN_DV, False, 1.0)
        kb = k3.astype(BF16)
        gram = _bmm_nt(kb, kb)
        qk = _bmm_nt(q3.astype(BF16), kb)

        def one_direction(d, keep_incl, keep_strict, total_row, u_ref, w_ref, qe_ref, kd_ref, qk_ref):
            lane = d * GDN_HEADS + h
            g_col = gc[:, :, lane:lane + 1]
            g_row = grow_ref[:, lane:lane + 1, :]
            beta = gc[:, :, 2 * GDN_HEADS + lane:2 * GDN_HEADS + lane + 1]
            decay = jnp.where(keep_incl, jnp.exp(jnp.where(keep_incl, g_col - g_row, 0.0)), 0.0)
            l_mat = jnp.where(keep_strict, gram * beta * decay, 0.0)
            e_g = jnp.exp(g_col)
            rhs = jnp.concatenate([v3 * beta, k3 * (beta * e_g)], axis=-1)
            sol = _unit_tri_solve(l_mat, rhs, on_diag, eye)
            g_total = g_col[:, total_row:total_row + 1, :]
            u_ref[:, hs] = sol[:, :, :GDN_DV].reshape(seg, GDN_DV)
            w_ref[:, hs] = sol[:, :, GDN_DV:].astype(BF16).reshape(seg, GDN_DK)
            qe_ref[:, hs] = (q3 * e_g).astype(BF16).reshape(seg, GDN_DK)
            kd_ref[:, hs] = (k3 * jnp.exp(g_total - g_col)).astype(BF16).reshape(seg, GDN_DK)
            qk_ref[h] = (qk * decay).astype(BF16).reshape(seg, CHUNK)
            et_ref[:, lane:lane + 1, :] = jnp.broadcast_to(jnp.exp(g_total), (K, 1, GDN_DV))

        one_direction(0, row >= col, row > col, CHUNK - 1, uf_ref, wf_ref, qef_ref, kdf_ref, qkf_ref)
        one_direction(1, row <= col, row < col, 0, ub_ref, wb_ref, qeb_ref, kdb_ref, qkb_ref)


def _gdn_scan_kernel(uf_ref, wf_ref, qef_ref, kdf_ref, qkf_ref, etf_ref,
                     ub_ref, wb_ref, qeb_ref, kdb_ref, qkb_ref, etb_ref,
                     of_ref, ob_ref, st_ref):
    seg = uf_ref.shape[0]
    K = seg // CHUNK

    @pl.when(pl.program_id(1) == 0)
    def _():
        st_ref[...] = jnp.zeros(st_ref.shape, F32)

    dirs = ((uf_ref, wf_ref, qef_ref, kdf_ref, qkf_ref, etf_ref, of_ref),
            (ub_ref, wb_ref, qeb_ref, kdb_ref, qkb_ref, etb_ref, ob_ref))

    def step(t, carry):
        chains = []
        for d, (u_ref, w_ref, qe_ref, kd_ref, qk_ref, et_ref, o_ref) in enumerate(dirs):
            c = t if d == 0 else K - 1 - t
            rows = pl.ds(pl.multiple_of(c * CHUNK, CHUNK), CHUNK)
            for h in range(GDN_HEADS):
                chains.append((d * GDN_HEADS + h, h, c, rows, slice(h * GDN_DK, (h + 1) * GDN_DK),
                               u_ref, w_ref, qe_ref, kd_ref, qk_ref, et_ref, o_ref))
        sb = [st_ref[ch[0]].astype(BF16) for ch in chains]
        v_new = [u_ref[rows, hs] - _dot(w_ref[rows, hs], s)
                 for s, (_, _, _, rows, hs, u_ref, w_ref, *_) in zip(sb, chains)]
        o_inter = [_dot(qe_ref[rows, hs], s)
                   for s, (_, _, _, rows, hs, _, _, qe_ref, *_) in zip(sb, chains)]
        vb = [v.astype(BF16) for v in v_new]
        for v, oi, (lane, h, c, rows, hs, _, _, _, kd_ref, qk_ref, et_ref, o_ref) in zip(vb, o_inter, chains):
            o_ref[rows, hs] = oi + _dot(qk_ref[h, rows, :], v)
            st_ref[lane] = st_ref[lane] * et_ref[c, lane:lane + 1, :] + _dot_tn(kd_ref[rows, hs], v)
        return carry

    lax.fori_loop(0, K, step, 0)


def _gdn(zg, conv_w, gcol, grow, seg):
    B, S, _ = zg.shape
    H = GDN_HEADS
    nseg = S // seg
    K = seg // CHUNK
    hb = seg // GDN_HALO
    last_hb = S // GDN_HALO - 1

    wide = lambda dt: jax.ShapeDtypeStruct((B, S, H * GDN_DK), dt)
    qk_shape = jax.ShapeDtypeStruct((B, H, S, CHUNK), BF16)
    et_shape = jax.ShapeDtypeStruct((B, S // CHUNK, 2 * H, GDN_DV), F32)
    dir_shapes = [wide(F32), wide(BF16), wide(BF16), wide(BF16), qk_shape]
    seg_spec = lambda imap: pl.BlockSpec((None, seg, H * GDN_DK), imap)
    qk_spec = lambda imap: pl.BlockSpec((None, H, seg, CHUNK), imap)
    et_spec = lambda imap: pl.BlockSpec((None, K, 2 * H, GDN_DV), imap)

    here3 = lambda b, j: (b, j, 0)
    here4 = lambda b, j: (b, 0, j, 0)
    dir_specs = [seg_spec(here3)] * 4 + [qk_spec(here4)]
    prep = pl.pallas_call(
        _gdn_prep_kernel, grid=(B, nseg),
        in_specs=[pl.BlockSpec((None, seg, GDN_QKV), here3),
                  pl.BlockSpec((None, GDN_HALO, GDN_QKV), lambda b, j: (b, jnp.maximum(j * hb - 1, 0), 0)),
                  pl.BlockSpec((None, GDN_HALO, GDN_QKV), lambda b, j: (b, jnp.minimum((j + 1) * hb, last_hb), 0)),
                  pl.BlockSpec((GDN_CONV, GDN_QKV), lambda b, j: (0, 0)),
                  pl.BlockSpec((None, seg, 4 * H), here3),
                  pl.BlockSpec((None, K, 4 * H, CHUNK), lambda b, j: (b, j, 0, 0))],
        out_specs=dir_specs + dir_specs + [et_spec(lambda b, j: (b, j, 0, 0))],
        out_shape=dir_shapes + dir_shapes + [et_shape],
        scratch_shapes=[pltpu.VMEM((seg + 2 * GDN_HALO, GDN_QKV), F32)],
        name="gdn_prep",
        compiler_params=pltpu.CompilerParams(dimension_semantics=("parallel", "parallel"),
                                             vmem_limit_bytes=VMEM_LIMIT_BYTES),
    )(zg, zg, zg, conv_w, gcol, grow)
    fwd, bwd, et = prep[:5], prep[5:10], prep[10]

    back3 = lambda b, j: (b, nseg - 1 - j, 0)
    back4 = lambda b, j: (b, 0, nseg - 1 - j, 0)
    return pl.pallas_call(
        _gdn_scan_kernel, grid=(B, nseg),
        in_specs=[seg_spec(here3)] * 4 + [qk_spec(here4), et_spec(lambda b, j: (b, j, 0, 0))]
                 + [seg_spec(back3)] * 4 + [qk_spec(back4), et_spec(lambda b, j: (b, nseg - 1 - j, 0, 0))],
        out_specs=[seg_spec(here3), seg_spec(back3)],
        out_shape=[wide(F32), wide(F32)],
        scratch_shapes=[pltpu.VMEM((2 * H, GDN_DK, GDN_DV), F32)],
        name="gdn_scan",
        compiler_params=pltpu.CompilerParams(dimension_semantics=("parallel", "arbitrary"),
                                             vmem_limit_bytes=VMEM_LIMIT_BYTES),
    )(*fwd, et, *bwd, et)


def _out_kernel(x_ref, oa_ref, of_ref, ob_ref, gg_ref, sa_ref, sb_ref, gnw_ref, wpa_ref, wpb_ref, wo_ref,
                n2_ref, w1_ref, w2_ref, nf_ref, y_ref):
    parts = []
    for h in range(GDN_HEADS):
        hs = slice(h * GDN_DV, (h + 1) * GDN_DV)
        o = of_ref[:, hs] + ob_ref[:, hs]
        parts.append((_rms(o, gnw_ref[...]) * gg_ref[:, hs].astype(F32)).astype(BF16))
    o_b = jnp.concatenate(parts, axis=-1)

    merged = (sa_ref[...].astype(F32) * _dot(oa_ref[...], wpa_ref[...])
              + sb_ref[...].astype(F32) * _dot(o_b, wpb_ref[...]))
    x1 = x_ref[...] + _dot(merged.astype(BF16), wo_ref[...])
    h2 = _rms(x1, n2_ref[...]).astype(BF16)
    a = jnp.maximum(_dot(h2, w1_ref[...]), 0.0)
    x2 = x1 + _dot((a * a).astype(BF16), w2_ref[...])
    y_ref[...] = _rms(x2, nf_ref[...])


def _out_proj(x2, oa, o_f, o_b, gg, sa, sb, w, tm):
    T = x2.shape[0]

    def full(a):
        return pl.BlockSpec(a.shape, lambda i: (0,) * a.ndim, pipeline_mode=pl.Buffered(1))

    def rows(width):
        return pl.BlockSpec((tm, width), lambda i: (i, 0))

    consts = [w["gnw"], w["w_pa"], w["w_pb"], w["w_o"], w["n2"], w["w_1"], w["w_2"], w["nf"]]
    return pl.pallas_call(
        _out_kernel, grid=(T // tm,),
        in_specs=[rows(D_MODEL), rows(MLA_WIDTH), rows(GDN_WIDTH), rows(GDN_WIDTH), rows(GDN_WIDTH),
                  rows(D_MODEL), rows(D_MODEL)] + [full(a) for a in consts],
        out_specs=rows(D_MODEL),
        out_shape=jax.ShapeDtypeStruct((T, D_MODEL), F32),
        name="merge_mlp",
        compiler_params=pltpu.CompilerParams(dimension_semantics=("parallel",),
                                             vmem_limit_bytes=VMEM_LIMIT_BYTES),
    )(x2, oa, o_f, o_b, gg, sa, sb, *consts)


def _prepare_weights(norm1_w, w_in, q_norm_w, w_uq, kv_norm_w, w_ukv, a_log_f, dt_bias_f, a_log_b,
                     dt_bias_b, gdn_norm_w, w_proj_a, w_proj_b, w_out, norm2_w, w_ff1, w_ff2, final_norm_w):
    H = GDN_HEADS
    o_kv = MLA_Q_LORA
    o_kr = o_kv + MLA_KV_LORA
    o_gdn = o_kr + MLA_ROPE
    o_gab = o_gdn + GDN_QKV
    o_gg = o_gab + 4 * H
    o_ga = o_gg + GDN_WIDTH
    o_gb = o_ga + D_MODEL
    zeros = lambda n: jnp.zeros((D_MODEL, n), F32)
    w_kr = jnp.concatenate([zeros(MLA_NOPE), w_in[:, o_kr:o_gdn], zeros(HEAD_PAD - MLA_NOPE - MLA_ROPE)], 1)
    w_gab = w_in[:, o_gab:o_gg]
    w_small = jnp.concatenate([w_in[:, :o_kr], w_kr, w_gab, zeros(HEAD_PAD - 4 * H)], 1)

    dq = MLA_NOPE + MLA_ROPE
    w_uq_p = jnp.pad(w_uq.reshape(MLA_Q_LORA, MLA_HEADS, dq), ((0, 0), (0, 0), (0, HEAD_PAD - dq)))
    w_ukv_r = w_ukv.reshape(MLA_KV_LORA, MLA_HEADS, MLA_NOPE + MLA_V)
    w_uk_p = jnp.pad(w_ukv_r[:, :, :MLA_NOPE], ((0, 0), (0, 0), (0, HEAD_PAD - MLA_NOPE)))
    w_v = w_ukv_r[:, :, MLA_NOPE:]

    inv_freq = 1.0 / (ROPE_THETA ** (jnp.arange(0, MLA_ROPE, 2, dtype=F32) / MLA_ROPE))
    z16 = jnp.zeros((ROPE_HALF,), F32)
    one16 = jnp.ones((ROPE_HALF,), F32)
    lane = lambda lo, hi: jnp.concatenate([jnp.zeros((MLA_NOPE,), F32), lo, hi,
                                           jnp.zeros((HEAD_PAD - MLA_NOPE - MLA_ROPE,), F32)])
    rope_c = jnp.stack([lane(inv_freq, inv_freq), lane(-one16, z16), lane(z16, one16)])

    bias = jnp.concatenate([dt_bias_f, dt_bias_b, jnp.zeros((2 * H,), F32)])
    neg_a = jnp.concatenate([-jnp.exp(a_log_f), -jnp.exp(a_log_b), jnp.zeros((2 * H,), F32)])
    g_row_c = jnp.stack([bias, neg_a], axis=1)
    g_col_c = jnp.pad(jnp.stack([bias, neg_a]), ((0, 0), (0, HEAD_PAD - 4 * H)))

    return {
        "n1": norm1_w.reshape(1, D_MODEL),
        "w_small": w_small.astype(BF16),
        "w_gdn": w_in[:, o_gdn:o_gab].astype(BF16),
        "w_gg": w_in[:, o_gg:o_ga].astype(BF16),
        "w_ga": w_in[:, o_ga:o_gb].astype(BF16),
        "w_gb": w_in[:, o_gb:].astype(BF16),
        "w_gt": w_gab.T.astype(BF16),
        "qn": q_norm_w.reshape(1, MLA_Q_LORA),
        "w_uq": w_uq_p.reshape(MLA_Q_LORA, MLA_HEADS * HEAD_PAD).astype(BF16),
        "kvn": kv_norm_w.reshape(1, MLA_KV_LORA),
        "w_uk": w_uk_p.reshape(MLA_KV_LORA, MLA_HEADS * HEAD_PAD).astype(BF16),
        "w_v": w_v.reshape(MLA_KV_LORA, MLA_WIDTH).astype(BF16),
        "rope_c": rope_c,
        "g_col_c": g_col_c,
        "g_row_c": g_row_c,
        "gnw": gdn_norm_w.reshape(1, GDN_DV),
        "w_pa": w_proj_a.astype(BF16),
        "w_pb": w_proj_b.astype(BF16),
        "w_o": w_out.astype(BF16),
        "n2": norm2_w.reshape(1, D_MODEL),
        "w_1": w_ff1.astype(BF16),
        "w_2": w_ff2.astype(BF16),
        "nf": final_norm_w.reshape(1, D_MODEL),
    }


def _layer(x, positions, conv_w, w):
    B, S, _ = x.shape
    T = B * S
    tm = min(512, S)
    x2 = x.reshape(T, D_MODEL)
    q, k, v, zg, gg, sa, sb, gcol, grow = _in_proj(x2, positions.reshape(T, 1), w, tm)

    o_a = _attention(q.reshape(B, S, -1), k.reshape(B, S, -1), v.reshape(B, S, -1), min(256, S))

    grow_c = grow.reshape(4 * GDN_HEADS, B, S // CHUNK, CHUNK).transpose(1, 2, 0, 3)
    o_f, o_b = _gdn(zg.reshape(B, S, GDN_QKV), conv_w, gcol.reshape(B, S, 4 * GDN_HEADS), grow_c, min(512, S))

    y = _out_proj(x2, o_a.reshape(T, MLA_WIDTH), o_f.reshape(T, GDN_WIDTH), o_b.reshape(T, GDN_WIDTH),
                  gg, sa, sb, w, tm)
    return y.reshape(B, S, D_MODEL)


def kernel(x, positions, norm1_w, w_in, q_norm_w, w_uq, kv_norm_w, w_ukv, conv_w, a_log_f, dt_bias_f,
           a_log_b, dt_bias_b, gdn_norm_w, w_proj_a, w_proj_b, w_out, norm2_w, w_ff1, w_ff2,
           final_norm_w):
    assert norm1_w.shape[0] == 1, "single-layer block"
    w = _prepare_weights(norm1_w[0], w_in[0], q_norm_w[0], w_uq[0], kv_norm_w[0], w_ukv[0], a_log_f[0],
                         dt_bias_f[0], a_log_b[0], dt_bias_b[0], gdn_norm_w[0], w_proj_a[0], w_proj_b[0],
                         w_out[0], norm2_w[0], w_ff1[0], w_ff2[0], final_norm_w)
    return _layer(x, positions, conv_w[0], w)
```

```python
import functools
import math

import jax
import jax.numpy as jnp
from jax import lax
from jax.experimental import pallas as pl
from jax.experimental.pallas import tpu as pltpu

F32 = jnp.float32
BF16 = jnp.bfloat16

D_MODEL = 1024
MLA_HEADS = 8
MLA_Q_LORA = 256
MLA_KV_LORA = 128
MLA_NOPE = 64
MLA_ROPE = 32
MLA_V = 64
ROPE_THETA = 10000.0
GDN_HEADS = 4
GDN_DK = 128
GDN_DV = 128
GDN_CONV = 5
CHUNK = 64
SUB = 16
D_FF = 4 * D_MODEL
EPS = 1e-6

MLA_WIDTH = MLA_HEADS * MLA_V
GDN_QK = GDN_HEADS * GDN_DK
GDN_WIDTH = GDN_HEADS * GDN_DV
GDN_QKV = 2 * GDN_QK + GDN_WIDTH
HEAD_PAD = 128
ROPE_HALF = MLA_ROPE // 2

VMEM_LIMIT_BYTES = 56 * 1024 * 1024


def _dot(a, b):
    return jnp.dot(a, b, preferred_element_type=F32)


def _dot_nt(a, b):
    return lax.dot_general(a, b, (((1,), (1,)), ((), ())), preferred_element_type=F32)


def _dot_tn(a, b):
    return lax.dot_general(a, b, (((0,), (0,)), ((), ())), preferred_element_type=F32)


def _sigmoid(x):
    return 1.0 / (1.0 + jnp.exp(-x))


def _silu(x):
    return x * _sigmoid(x)


def _softplus(x):
    return jnp.maximum(x, 0.0) + jnp.log(1.0 + jnp.exp(-jnp.abs(x)))


def _rms(x, w):
    return x * lax.rsqrt(jnp.mean(x * x, axis=-1, keepdims=True) + EPS) * w


def _chunk_cumsum(x, axis, reverse):
    n = x.shape[axis]
    pos = lax.broadcasted_iota(jnp.int32, x.shape, axis) % CHUNK
    s = 1
    while s < CHUNK:
        if reverse:
            shifted = pltpu.roll(x, n - s, axis)
            x = x + jnp.where(pos < CHUNK - s, shifted, 0.0)
        else:
            shifted = pltpu.roll(x, s, axis)
            x = x + jnp.where(pos >= s, shifted, 0.0)
        s *= 2
    return x


def _decay_beta(z, bias, neg_a, axis):
    idx = lax.broadcasted_iota(jnp.int32, z.shape, 1 - axis)
    g = neg_a * _softplus(z + bias)
    g_f = _chunk_cumsum(jnp.where(idx < GDN_HEADS, g, 0.0), axis, reverse=False)
    g_b = _chunk_cumsum(jnp.where((idx >= GDN_HEADS) & (idx < 2 * GDN_HEADS), g, 0.0), axis, reverse=True)
    return jnp.where(idx < GDN_HEADS, g_f, jnp.where(idx < 2 * GDN_HEADS, g_b, _sigmoid(z)))


def _rope(t, cos, sin_lo, sin_hi):
    return (t * cos + pltpu.roll(t, HEAD_PAD - ROPE_HALF, 1) * sin_lo
            + pltpu.roll(t, ROPE_HALF, 1) * sin_hi)


def _in_proj_kernel(x_ref, pos_ref, n1_ref, wsm_ref, wgdn_ref, wgg_ref, wga_ref, wgb_ref, wgt_ref,
                    qn_ref, wuq_ref, kvn_ref, wuk_ref, wv_ref, vone_ref, ropec_ref, gcc_ref, gcr_ref,
                    q_out, k_out, v_out, zg_out, gg_out, sa_out, sb_out, gcol_out, grow_out):
    hb = _rms(x_ref[...], n1_ref[...]).astype(BF16)

    zs = _dot(hb, wsm_ref[...])
    c_q = zs[:, :MLA_Q_LORA]
    c_kv = zs[:, MLA_Q_LORA:MLA_Q_LORA + MLA_KV_LORA]
    k_r = zs[:, 384:512]
    z_gab = zs[:, 512:640]

    ang = pos_ref[...].astype(F32) * ropec_ref[0:1, :]
    cos = jnp.cos(ang)
    sin = jnp.sin(ang)
    sin_lo = sin * ropec_ref[1:2, :]
    sin_hi = sin * ropec_ref[2:3, :]

    qf = _dot(_rms(c_q, qn_ref[...]).astype(BF16), wuq_ref[...])
    scale = math.log2(math.e) / math.sqrt(MLA_NOPE + MLA_ROPE)
    for h in range(MLA_HEADS):
        sl = slice(h * HEAD_PAD, (h + 1) * HEAD_PAD)
        q_out[:, sl] = (_rope(qf[:, sl], cos, sin_lo, sin_hi) * scale).astype(BF16)

    ckv = _rms(c_kv, kvn_ref[...]).astype(BF16)
    kn = _dot(ckv, wuk_ref[...])
    kr = _rope(k_r, cos, sin_lo, sin_hi)
    for h in range(MLA_HEADS):
        sl = slice(h * HEAD_PAD, (h + 1) * HEAD_PAD)
        k_out[:, sl] = (kn[:, sl] + kr).astype(BF16)
    v_out[...] = (_dot(ckv, wv_ref[...]) + vone_ref[...]).astype(BF16)

    zg_out[...] = _dot(hb, wgdn_ref[...]).astype(BF16)
    gg_out[...] = _silu(_dot(hb, wgg_ref[...])).astype(BF16)
    sa_out[...] = _sigmoid(_dot(hb, wga_ref[...])).astype(BF16)
    sb_out[...] = _sigmoid(_dot(hb, wgb_ref[...])).astype(BF16)

    col = _decay_beta(z_gab, gcc_ref[0:1, :], gcc_ref[1:2, :], axis=0)
    gcol_out[...] = col[:, :4 * GDN_HEADS]
    z_t = _dot_nt(wgt_ref[...], hb)
    grow_out[...] = _decay_beta(z_t, gcr_ref[:, 0:1], gcr_ref[:, 1:2], axis=1)


def _in_proj(x2, pos2, w, tm):
    T = x2.shape[0]
    grid = (T // tm,)

    def full(a):
        return pl.BlockSpec(a.shape, lambda i: (0,) * a.ndim, pipeline_mode=pl.Buffered(1))

    def rows(width):
        return pl.BlockSpec((tm, width), lambda i: (i, 0))

    ins = [x2, pos2, w["n1"], w["w_small"], w["w_gdn"], w["w_gg"], w["w_ga"], w["w_gb"], w["w_gt"],
           w["qn"], w["w_uq"], w["kvn"], w["w_uk"], w["w_v"], w["v_one"], w["rope_c"], w["g_col_c"], w["g_row_c"]]
    in_specs = [rows(D_MODEL), rows(1)] + [full(a) for a in ins[2:]]
    out_shape = [
        jax.ShapeDtypeStruct((T, MLA_HEADS * HEAD_PAD), BF16),
        jax.ShapeDtypeStruct((T, MLA_HEADS * HEAD_PAD), BF16),
        jax.ShapeDtypeStruct((T, MLA_HEADS * HEAD_PAD), BF16),
        jax.ShapeDtypeStruct((T, GDN_QKV), BF16),
        jax.ShapeDtypeStruct((T, GDN_WIDTH), BF16),
        jax.ShapeDtypeStruct((T, D_MODEL), BF16),
        jax.ShapeDtypeStruct((T, D_MODEL), BF16),
        jax.ShapeDtypeStruct((T, 4 * GDN_HEADS), F32),
        jax.ShapeDtypeStruct((4 * GDN_HEADS, T), F32),
    ]
    out_specs = [rows(MLA_HEADS * HEAD_PAD), rows(MLA_HEADS * HEAD_PAD), rows(MLA_HEADS * HEAD_PAD), rows(GDN_QKV),
                 rows(GDN_WIDTH), rows(D_MODEL), rows(D_MODEL), rows(4 * GDN_HEADS),
                 pl.BlockSpec((4 * GDN_HEADS, tm), lambda i: (0, i))]
    return pl.pallas_call(
        _in_proj_kernel, grid=grid, in_specs=in_specs, out_specs=out_specs, out_shape=out_shape,
        name="in_proj",
        compiler_params=pltpu.CompilerParams(dimension_semantics=("parallel",),
                                             vmem_limit_bytes=VMEM_LIMIT_BYTES),
    )(*ins)


ATTN_GROUP = 2


def _attn_kernel(q_ref, k_ref, v_ref, o_ref, *, tq):
    S = q_ref.shape[0]

    def q_tile(i, carry):
        rows = pl.ds(pl.multiple_of(i * tq, tq), tq)
        scores = [_dot_nt(q_ref[rows, j * HEAD_PAD:(j + 1) * HEAD_PAD], k_ref[:, j * HEAD_PAD:(j + 1) * HEAD_PAD])
                  for j in range(ATTN_GROUP)]
        outs = []
        for j, s in enumerate(scores):
            p = jnp.exp2((s - jnp.max(s, axis=-1, keepdims=True)).astype(BF16))
            o = _dot(p, v_ref[:, j * HEAD_PAD:(j + 1) * HEAD_PAD])
            outs.append(o[:, :MLA_V] / o[:, MLA_V:MLA_V + 1])
        o_ref[rows, :] = jnp.concatenate(outs, axis=-1).astype(BF16)
        return carry

    lax.fori_loop(0, S // tq, q_tile, 0, unroll=2)


def _attention(q, k, v, tq):
    B, S, _ = q.shape
    group = pl.BlockSpec((None, S, ATTN_GROUP * HEAD_PAD), lambda b, g: (b, 0, g))
    return pl.pallas_call(
        functools.partial(_attn_kernel, tq=tq), grid=(B, MLA_HEADS // ATTN_GROUP),
        in_specs=[group, group, group],
        out_specs=pl.BlockSpec((None, S, ATTN_GROUP * MLA_V), lambda b, g: (b, 0, g)),
        out_shape=jax.ShapeDtypeStruct((B, S, MLA_WIDTH), BF16),
        name="mla_attention",
        compiler_params=pltpu.CompilerParams(dimension_semantics=("parallel", "parallel"),
                                             vmem_limit_bytes=VMEM_LIMIT_BYTES),
    )(q, k, v)


def _bmm(a, b):
    return jnp.einsum("kij,kjl->kil", a, b, preferred_element_type=F32)


def _bmm_nt(a, b):
    return jnp.einsum("kid,kjd->kij", a, b, preferred_element_type=F32)


def _unit_tri_solve(l_mat, rhs, on_diag, eye):
    n1 = jnp.where(on_diag, -l_mat, 0.0)
    l_off = jnp.where(on_diag, 0.0, l_mat)
    n2 = _bmm(n1, n1)
    d = eye + n1
    d = d + _bmm(d, n2)
    n4 = _bmm(n2, n2)
    d = d + _bmm(d, n4)
    n8 = _bmm(n4, n4)
    d = d + _bmm(d, n8)
    db = d.astype(BF16)
    m = _bmm(db, l_off.astype(BF16))
    y0 = _bmm(db, rhs.astype(BF16))
    mb = m.astype(BF16)
    m2 = _bmm(mb, mb)
    z1 = y0 - _bmm(mb, y0.astype(BF16))
    return z1 + _bmm(m2.astype(BF16), z1.astype(BF16))


GDN_HALO = 16


def _gdn_prep_kernel(z_ref, zp_ref, zn_ref, cw_ref, gcol_ref, grow_ref,
                     uf_ref, wf_ref, qef_ref, kdf_ref, qkf_ref,
                     ub_ref, wb_ref, qeb_ref, kdb_ref, qkb_ref, et_ref, pad_s):
    seg = z_ref.shape[0]
    K = seg // CHUNK
    j = pl.program_id(1)
    halo = GDN_HALO

    pad_s[0:halo, :] = jnp.where(j == 0, 0.0, zp_ref[...].astype(F32))
    pad_s[halo:halo + seg, :] = z_ref[...].astype(F32)
    pad_s[halo + seg:, :] = jnp.where(j == pl.num_programs(1) - 1, 0.0, zn_ref[...].astype(F32))

    def conv(c0, l2, mult):
        acc = jnp.zeros((seg, GDN_DK), F32)
        for t in range(GDN_CONV):
            off = halo + t - GDN_CONV // 2
            acc = acc + pad_s[off:off + seg, c0:c0 + GDN_DK] * cw_ref[t:t + 1, c0:c0 + GDN_DK]
        y = _silu(acc)
        if l2:
            y = y * (lax.rsqrt(jnp.sum(y * y, axis=-1, keepdims=True) + EPS) * mult)
        return y.reshape(K, CHUNK, GDN_DK)

    row = lax.broadcasted_iota(jnp.int32, (CHUNK, CHUNK), 0)
    col = lax.broadcasted_iota(jnp.int32, (CHUNK, CHUNK), 1)
    on_diag = (row // SUB) == (col // SUB)
    eye = (row == col).astype(F32)
    gc = gcol_ref[...].reshape(K, CHUNK, 4 * GDN_HEADS)

    systems, rhs_all, stores = [], [], []
    for h in range(GDN_HEADS):
        hs = slice(h * GDN_DK, (h + 1) * GDN_DK)
        q3 = conv(h * GDN_DK, True, GDN_DK ** -0.5)
        k3 = conv(GDN_QK + h * GDN_DK, True, 1.0)
        v3 = conv(2 * GDN_QK + h * GDN_DV, False, 1.0)
        kb = k3.astype(BF16)
        gram = _bmm_nt(kb, kb)
        qk = _bmm_nt(q3.astype(BF16), kb)
        for d, keep_incl, keep_strict, total_row, refs in (
                (0, row >= col, row > col, CHUNK - 1, (uf_ref, wf_ref, qef_ref, kdf_ref, qkf_ref)),
                (1, row <= col, row < col, 0, (ub_ref, wb_ref, qeb_ref, kdb_ref, qkb_ref))):
            u_ref, w_ref, qe_ref, kd_ref, qk_ref = refs
            lane = d * GDN_HEADS + h
            g_col = gc[:, :, lane:lane + 1]
            g_row = grow_ref[:, lane:lane + 1, :]
            beta = gc[:, :, 2 * GDN_HEADS + lane:2 * GDN_HEADS + lane + 1]
            decay = jnp.where(keep_incl, jnp.exp(jnp.where(keep_incl, g_col - g_row, 0.0)), 0.0)
            e_g = jnp.exp(g_col)
            g_total = g_col[:, total_row:total_row + 1, :]
            systems.append(jnp.where(keep_strict, gram * beta * decay, 0.0))
            rhs_all.append(jnp.concatenate([v3 * beta, k3 * (beta * e_g)], axis=-1))
            stores.append((u_ref, w_ref, hs))
            qe_ref[:, hs] = (q3 * e_g).astype(BF16).reshape(seg, GDN_DK)
            kd_ref[:, hs] = (k3 * jnp.exp(g_total - g_col)).astype(BF16).reshape(seg, GDN_DK)
            qk_ref[h] = (qk * decay).astype(BF16).reshape(seg, CHUNK)
            et_ref[:, lane:lane + 1, :] = jnp.broadcast_to(jnp.exp(g_total), (K, 1, GDN_DV))

    sol = _unit_tri_solve(jnp.concatenate(systems, axis=0), jnp.concatenate(rhs_all, axis=0), on_diag, eye)
    for i, (u_ref, w_ref, hs) in enumerate(stores):
        part = sol[i * K:(i + 1) * K]
        u_ref[:, hs] = part[:, :, :GDN_DV].reshape(seg, GDN_DV)
        w_ref[:, hs] = part[:, :, GDN_DV:].astype(BF16).reshape(seg, GDN_DK)


def _gdn_scan_kernel(uf_ref, wf_ref, qef_ref, kdf_ref, qkf_ref, etf_ref,
                     ub_ref, wb_ref, qeb_ref, kdb_ref, qkb_ref, etb_ref,
                     of_ref, ob_ref, st_ref):
    seg = uf_ref.shape[0]
    K = seg // CHUNK

    @pl.when(pl.program_id(1) == 0)
    def _():
        st_ref[...] = jnp.zeros(st_ref.shape, F32)

    dirs = ((uf_ref, wf_ref, qef_ref, kdf_ref, qkf_ref, etf_ref, of_ref),
            (ub_ref, wb_ref, qeb_ref, kdb_ref, qkb_ref, etb_ref, ob_ref))

    def step(t, carry):
        chains = []
        for d, (u_ref, w_ref, qe_ref, kd_ref, qk_ref, et_ref, o_ref) in enumerate(dirs):
            c = t if d == 0 else K - 1 - t
            rows = pl.ds(pl.multiple_of(c * CHUNK, CHUNK), CHUNK)
            for h in range(GDN_HEADS):
                chains.append((d * GDN_HEADS + h, h, c, rows, slice(h * GDN_DK, (h + 1) * GDN_DK),
                               u_ref, w_ref, qe_ref, kd_ref, qk_ref, et_ref, o_ref))
        sb = [st_ref[ch[0]].astype(BF16) for ch in chains]
        v_new = [u_ref[rows, hs] - _dot(w_ref[rows, hs], s)
                 for s, (_, _, _, rows, hs, u_ref, w_ref, *_) in zip(sb, chains)]
        o_inter = [_dot(qe_ref[rows, hs], s)
                   for s, (_, _, _, rows, hs, _, _, qe_ref, *_) in zip(sb, chains)]
        vb = [v.astype(BF16) for v in v_new]
        for v, oi, (lane, h, c, rows, hs, _, _, _, kd_ref, qk_ref, et_ref, o_ref) in zip(vb, o_inter, chains):
            o_ref[rows, hs] = oi + _dot(qk_ref[h, rows, :], v)
            st_ref[lane] = st_ref[lane] * et_ref[c, lane:lane + 1, :] + _dot_tn(kd_ref[rows, hs], v)
        return carry

    lax.fori_loop(0, K, step, 0)


def _gdn(zg, conv_w, gcol, grow, seg):
    B, S, _ = zg.shape
    H = GDN_HEADS
    nseg = S // seg
    K = seg // CHUNK
    hb = seg // GDN_HALO
    last_hb = S // GDN_HALO - 1

    wide = lambda dt: jax.ShapeDtypeStruct((B, S, H * GDN_DK), dt)
    qk_shape = jax.ShapeDtypeStruct((B, H, S, CHUNK), BF16)
    et_shape = jax.ShapeDtypeStruct((B, S // CHUNK, 2 * H, GDN_DV), F32)
    dir_shapes = [wide(F32), wide(BF16), wide(BF16), wide(BF16), qk_shape]
    seg_spec = lambda imap: pl.BlockSpec((None, seg, H * GDN_DK), imap)
    qk_spec = lambda imap: pl.BlockSpec((None, H, seg, CHUNK), imap)
    et_spec = lambda imap: pl.BlockSpec((None, K, 2 * H, GDN_DV), imap)

    here3 = lambda b, j: (b, j, 0)
    here4 = lambda b, j: (b, 0, j, 0)
    dir_specs = [seg_spec(here3)] * 4 + [qk_spec(here4)]
    prep = pl.pallas_call(
        _gdn_prep_kernel, grid=(B, nseg),
        in_specs=[pl.BlockSpec((None, seg, GDN_QKV), here3),
                  pl.BlockSpec((None, GDN_HALO, GDN_QKV), lambda b, j: (b, jnp.maximum(j * hb - 1, 0), 0)),
                  pl.BlockSpec((None, GDN_HALO, GDN_QKV), lambda b, j: (b, jnp.minimum((j + 1) * hb, last_hb), 0)),
                  pl.BlockSpec((GDN_CONV, GDN_QKV), lambda b, j: (0, 0)),
                  pl.BlockSpec((None, seg, 4 * H), here3),
                  pl.BlockSpec((None, K, 4 * H, CHUNK), lambda b, j: (b, j, 0, 0))],
        out_specs=dir_specs + dir_specs + [et_spec(lambda b, j: (b, j, 0, 0))],
        out_shape=dir_shapes + dir_shapes + [et_shape],
        scratch_shapes=[pltpu.VMEM((seg + 2 * GDN_HALO, GDN_QKV), F32)],
        name="gdn_prep",
        compiler_params=pltpu.CompilerParams(dimension_semantics=("parallel", "parallel"),
                                             vmem_limit_bytes=VMEM_LIMIT_BYTES),
    )(zg, zg, zg, conv_w, gcol, grow)
    fwd, bwd, et = prep[:5], prep[5:10], prep[10]

    back3 = lambda b, j: (b, nseg - 1 - j, 0)
    back4 = lambda b, j: (b, 0, nseg - 1 - j, 0)
    return pl.pallas_call(
        _gdn_scan_kernel, grid=(B, nseg),
        in_specs=[seg_spec(here3)] * 4 + [qk_spec(here4), et_spec(lambda b, j: (b, j, 0, 0))]
                 + [seg_spec(back3)] * 4 + [qk_spec(back4), et_spec(lambda b, j: (b, nseg - 1 - j, 0, 0))],
        out_specs=[seg_spec(here3), seg_spec(back3)],
        out_shape=[wide(F32), wide(F32)],
        scratch_shapes=[pltpu.VMEM((2 * H, GDN_DK, GDN_DV), F32)],
        name="gdn_scan",
        compiler_params=pltpu.CompilerParams(dimension_semantics=("parallel", "arbitrary"),
                                             vmem_limit_bytes=VMEM_LIMIT_BYTES),
    )(*fwd, et, *bwd, et)


def _out_kernel(x_ref, oa_ref, of_ref, ob_ref, gg_ref, sa_ref, sb_ref, gnw_ref, wpa_ref, wpb_ref, wo_ref,
                n2_ref, w1_ref, w2_ref, nf_ref, y_ref):
    parts = []
    for h in range(GDN_HEADS):
        hs = slice(h * GDN_DV, (h + 1) * GDN_DV)
        o = of_ref[:, hs] + ob_ref[:, hs]
        parts.append((_rms(o, gnw_ref[...]) * gg_ref[:, hs].astype(F32)).astype(BF16))
    o_b = jnp.concatenate(parts, axis=-1)

    merged = (sa_ref[...].astype(F32) * _dot(oa_ref[...], wpa_ref[...])
              + sb_ref[...].astype(F32) * _dot(o_b, wpb_ref[...]))
    x1 = x_ref[...] + _dot(merged.astype(BF16), wo_ref[...])
    h2 = _rms(x1, n2_ref[...]).astype(BF16)
    a = jnp.maximum(_dot(h2, w1_ref[...]), 0.0)
    x2 = x1 + _dot((a * a).astype(BF16), w2_ref[...])
    y_ref[...] = _rms(x2, nf_ref[...])


def _out_proj(x2, oa, o_f, o_b, gg, sa, sb, w, tm):
    T = x2.shape[0]

    def full(a):
        return pl.BlockSpec(a.shape, lambda i: (0,) * a.ndim, pipeline_mode=pl.Buffered(1))

    def rows(width):
        return pl.BlockSpec((tm, width), lambda i: (i, 0))

    consts = [w["gnw"], w["w_pa"], w["w_pb"], w["w_o"], w["n2"], w["w_1"], w["w_2"], w["nf"]]
    return pl.pallas_call(
        _out_kernel, grid=(T // tm,),
        in_specs=[rows(D_MODEL), rows(MLA_WIDTH), rows(GDN_WIDTH), rows(GDN_WIDTH), rows(GDN_WIDTH),
                  rows(D_MODEL), rows(D_MODEL)] + [full(a) for a in consts],
        out_specs=rows(D_MODEL),
        out_shape=jax.ShapeDtypeStruct((T, D_MODEL), F32),
        name="merge_mlp",
        compiler_params=pltpu.CompilerParams(dimension_semantics=("parallel",),
                                             vmem_limit_bytes=VMEM_LIMIT_BYTES),
    )(x2, oa, o_f, o_b, gg, sa, sb, *consts)


def _prepare_weights(norm1_w, w_in, q_norm_w, w_uq, kv_norm_w, w_ukv, a_log_f, dt_bias_f, a_log_b,
                     dt_bias_b, gdn_norm_w, w_proj_a, w_proj_b, w_out, norm2_w, w_ff1, w_ff2, final_norm_w):
    H = GDN_HEADS
    o_kv = MLA_Q_LORA
    o_kr = o_kv + MLA_KV_LORA
    o_gdn = o_kr + MLA_ROPE
    o_gab = o_gdn + GDN_QKV
    o_gg = o_gab + 4 * H
    o_ga = o_gg + GDN_WIDTH
    o_gb = o_ga + D_MODEL
    zeros = lambda n: jnp.zeros((D_MODEL, n), F32)
    w_kr = jnp.concatenate([zeros(MLA_NOPE), w_in[:, o_kr:o_gdn], zeros(HEAD_PAD - MLA_NOPE - MLA_ROPE)], 1)
    w_gab = w_in[:, o_gab:o_gg]
    w_small = jnp.concatenate([w_in[:, :o_kr], w_kr, w_gab, zeros(HEAD_PAD - 4 * H)], 1)

    dq = MLA_NOPE + MLA_ROPE
    w_uq_p = jnp.pad(w_uq.reshape(MLA_Q_LORA, MLA_HEADS, dq), ((0, 0), (0, 0), (0, HEAD_PAD - dq)))
    w_ukv_r = w_ukv.reshape(MLA_KV_LORA, MLA_HEADS, MLA_NOPE + MLA_V)
    w_uk_p = jnp.pad(w_ukv_r[:, :, :MLA_NOPE], ((0, 0), (0, 0), (0, HEAD_PAD - MLA_NOPE)))
    w_v_p = jnp.pad(w_ukv_r[:, :, MLA_NOPE:], ((0, 0), (0, 0), (0, HEAD_PAD - MLA_V)))
    v_one = jnp.tile((jnp.arange(HEAD_PAD) == MLA_V).astype(F32), MLA_HEADS).reshape(1, MLA_HEADS * HEAD_PAD)

    inv_freq = 1.0 / (ROPE_THETA ** (jnp.arange(0, MLA_ROPE, 2, dtype=F32) / MLA_ROPE))
    z16 = jnp.zeros((ROPE_HALF,), F32)
    one16 = jnp.ones((ROPE_HALF,), F32)
    lane = lambda lo, hi: jnp.concatenate([jnp.zeros((MLA_NOPE,), F32), lo, hi,
                                           jnp.zeros((HEAD_PAD - MLA_NOPE - MLA_ROPE,), F32)])
    rope_c = jnp.stack([lane(inv_freq, inv_freq), lane(-one16, z16), lane(z16, one16)])

    bias = jnp.concatenate([dt_bias_f, dt_bias_b, jnp.zeros((2 * H,), F32)])
    neg_a = jnp.concatenate([-jnp.exp(a_log_f), -jnp.exp(a_log_b), jnp.zeros((2 * H,), F32)])
    g_row_c = jnp.stack([bias, neg_a], axis=1)
    g_col_c = jnp.pad(jnp.stack([bias, neg_a]), ((0, 0), (0, HEAD_PAD - 4 * H)))

    return {
        "n1": norm1_w.reshape(1, D_MODEL),
        "w_small": w_small.astype(BF16),
        "w_gdn": w_in[:, o_gdn:o_gab].astype(BF16),
        "w_gg": w_in[:, o_gg:o_ga].astype(BF16),
        "w_ga": w_in[:, o_ga:o_gb].astype(BF16),
        "w_gb": w_in[:, o_gb:].astype(BF16),
        "w_gt": w_gab.T.astype(BF16),
        "qn": q_norm_w.reshape(1, MLA_Q_LORA),
        "w_uq": w_uq_p.reshape(MLA_Q_LORA, MLA_HEADS * HEAD_PAD).astype(BF16),
        "kvn": kv_norm_w.reshape(1, MLA_KV_LORA),
        "w_uk": w_uk_p.reshape(MLA_KV_LORA, MLA_HEADS * HEAD_PAD).astype(BF16),
        "w_v": w_v_p.reshape(MLA_KV_LORA, MLA_HEADS * HEAD_PAD).astype(BF16),
        "v_one": v_one,
        "rope_c": rope_c,
        "g_col_c": g_col_c,
        "g_row_c": g_row_c,
        "gnw": gdn_norm_w.reshape(1, GDN_DV),
        "w_pa": w_proj_a.astype(BF16),
        "w_pb": w_proj_b.astype(BF16),
        "w_o": w_out.astype(BF16),
        "n2": norm2_w.reshape(1, D_MODEL),
        "w_1": w_ff1.astype(BF16),
        "w_2": w_ff2.astype(BF16),
        "nf": final_norm_w.reshape(1, D_MODEL),
    }


def _layer(x, positions, conv_w, w):
    B, S, _ = x.shape
    T = B * S
    tm = min(512, S)
    x2 = x.reshape(T, D_MODEL)
    q, k, v, zg, gg, sa, sb, gcol, grow = _in_proj(x2, positions.reshape(T, 1), w, tm)

    o_a = _attention(q.reshape(B, S, -1), k.reshape(B, S, -1), v.reshape(B, S, -1), min(512, S))

    grow_c = grow.reshape(4 * GDN_HEADS, B, S // CHUNK, CHUNK).transpose(1, 2, 0, 3)
    o_f, o_b = _gdn(zg.reshape(B, S, GDN_QKV), conv_w, gcol.reshape(B, S, 4 * GDN_HEADS), grow_c, min(512, S))

    y = _out_proj(x2, o_a.reshape(T, MLA_WIDTH), o_f.reshape(T, GDN_WIDTH), o_b.reshape(T, GDN_WIDTH),
                  gg, sa, sb, w, tm)
    return y.reshape(B, S, D_MODEL)


def kernel(x, positions, norm1_w, w_in, q_norm_w, w_uq, kv_norm_w, w_ukv, conv_w, a_log_f, dt_bias_f,
           a_log_b, dt_bias_b, gdn_norm_w, w_proj_a, w_proj_b, w_out, norm2_w, w_ff1, w_ff2,
           final_norm_w):
    assert norm1_w.shape[0] == 1, "single-layer block"
    w = _prepare_weights(norm1_w[0], w_in[0], q_norm_w[0], w_uq[0], kv_norm_w[0], w_ukv[0], a_log_f[0],
                         dt_bias_f[0], a_log_b[0], dt_bias_b[0], gdn_norm_w[0], w_proj_a[0], w_proj_b[0],
                         w_out[0], norm2_w[0], w_ff1[0], w_ff2[0], final_norm_w)
    return _layer(x, positions, conv_w[0], w)
```

```python
import functools
import math

import jax
import jax.numpy as jnp
from jax import lax
from jax.experimental import pallas as pl
from jax.experimental.pallas import tpu as pltpu

F32 = jnp.float32
BF16 = jnp.bfloat16

D_MODEL = 1024
MLA_HEADS = 8
MLA_Q_LORA = 256
MLA_KV_LORA = 128
MLA_NOPE = 64
MLA_ROPE = 32
MLA_V = 64
ROPE_THETA = 10000.0
GDN_HEADS = 4
GDN_DK = 128
GDN_DV = 128
GDN_CONV = 5
CHUNK = 64
SUB = 16
D_FF = 4 * D_MODEL
EPS = 1e-6

MLA_WIDTH = MLA_HEADS * MLA_V
GDN_QK = GDN_HEADS * GDN_DK
GDN_WIDTH = GDN_HEADS * GDN_DV
GDN_QKV = 2 * GDN_QK + GDN_WIDTH
HEAD_PAD = 128
ROPE_HALF = MLA_ROPE // 2

VMEM_LIMIT_BYTES = 56 * 1024 * 1024


def _dot(a, b):
    return jnp.dot(a, b, preferred_element_type=F32)


def _dot_nt(a, b):
    return lax.dot_general(a, b, (((1,), (1,)), ((), ())), preferred_element_type=F32)


def _dot_tn(a, b):
    return lax.dot_general(a, b, (((0,), (0,)), ((), ())), preferred_element_type=F32)


def _sigmoid(x):
    return 1.0 / (1.0 + jnp.exp(-x))


def _silu(x):
    return x * _sigmoid(x)


def _softplus(x):
    return jnp.maximum(x, 0.0) + jnp.log(1.0 + jnp.exp(-jnp.abs(x)))


def _rms(x, w):
    return x * lax.rsqrt(jnp.mean(x * x, axis=-1, keepdims=True) + EPS) * w


def _chunk_cumsum(x, axis, reverse):
    n = x.shape[axis]
    pos = lax.broadcasted_iota(jnp.int32, x.shape, axis) % CHUNK
    s = 1
    while s < CHUNK:
        if reverse:
            shifted = pltpu.roll(x, n - s, axis)
            x = x + jnp.where(pos < CHUNK - s, shifted, 0.0)
        else:
            shifted = pltpu.roll(x, s, axis)
            x = x + jnp.where(pos >= s, shifted, 0.0)
        s *= 2
    return x


def _decay_beta(z, bias, neg_a, axis):
    idx = lax.broadcasted_iota(jnp.int32, z.shape, 1 - axis)
    g = neg_a * _softplus(z + bias)
    g_f = _chunk_cumsum(jnp.where(idx < GDN_HEADS, g, 0.0), axis, reverse=False)
    g_b = _chunk_cumsum(jnp.where((idx >= GDN_HEADS) & (idx < 2 * GDN_HEADS), g, 0.0), axis, reverse=True)
    return jnp.where(idx < GDN_HEADS, g_f, jnp.where(idx < 2 * GDN_HEADS, g_b, _sigmoid(z)))


def _rope(t, cos, sin_lo, sin_hi):
    return (t * cos + pltpu.roll(t, HEAD_PAD - ROPE_HALF, 1) * sin_lo
            + pltpu.roll(t, ROPE_HALF, 1) * sin_hi)


def _in_proj_kernel(x_ref, pos_ref, n1_ref, wsm_ref, wgdn_ref, wgg_ref, wga_ref, wgb_ref, wgt_ref,
                    qn_ref, wuq_ref, kvn_ref, wuk_ref, wv_ref, vone_ref, ropec_ref, gcc_ref, gcr_ref,
                    q_out, k_out, v_out, zg_out, gg_out, sa_out, sb_out, gcol_out, grow_out):
    hb = _rms(x_ref[...], n1_ref[...]).astype(BF16)

    zs = _dot(hb, wsm_ref[...])
    c_q = zs[:, :MLA_Q_LORA]
    c_kv = zs[:, MLA_Q_LORA:MLA_Q_LORA + MLA_KV_LORA]
    k_r = zs[:, 384:512]
    z_gab = zs[:, 512:640]

    col = _decay_beta(z_gab, gcc_ref[0:1, :], gcc_ref[1:2, :], axis=0)
    gcol_out[...] = col[:, :4 * GDN_HEADS]
    z_t = _dot_nt(wgt_ref[...], hb)
    grow_out[...] = _decay_beta(z_t, gcr_ref[:, 0:1], gcr_ref[:, 1:2], axis=1)

    zg_out[...] = _dot(hb, wgdn_ref[...]).astype(BF16)
    gg_out[...] = _silu(_dot(hb, wgg_ref[...])).astype(BF16)
    sa_out[...] = _sigmoid(_dot(hb, wga_ref[...])).astype(BF16)
    sb_out[...] = _sigmoid(_dot(hb, wgb_ref[...])).astype(BF16)

    ang = pos_ref[...].astype(F32) * ropec_ref[0:1, :]
    cos = jnp.cos(ang)
    sin = jnp.sin(ang)
    sin_lo = sin * ropec_ref[1:2, :]
    sin_hi = sin * ropec_ref[2:3, :]

    qf = _dot(_rms(c_q, qn_ref[...]).astype(BF16), wuq_ref[...])
    scale = math.log2(math.e) / math.sqrt(MLA_NOPE + MLA_ROPE)
    for h in range(MLA_HEADS):
        sl = slice(h * HEAD_PAD, (h + 1) * HEAD_PAD)
        q_out[:, sl] = (_rope(qf[:, sl], cos, sin_lo, sin_hi) * scale).astype(BF16)

    ckv = _rms(c_kv, kvn_ref[...]).astype(BF16)
    kn = _dot(ckv, wuk_ref[...])
    kr = _rope(k_r, cos, sin_lo, sin_hi)
    for h in range(MLA_HEADS):
        sl = slice(h * HEAD_PAD, (h + 1) * HEAD_PAD)
        k_out[:, sl] = (kn[:, sl] + kr).astype(BF16)
    v_out[...] = (_dot(ckv, wv_ref[...]) + vone_ref[...]).astype(BF16)


def _in_proj(x2, pos2, w, tm):
    T = x2.shape[0]
    grid = (T // tm,)

    def full(a):
        return pl.BlockSpec(a.shape, lambda i: (0,) * a.ndim, pipeline_mode=pl.Buffered(1))

    def rows(width):
        return pl.BlockSpec((tm, width), lambda i: (i, 0))

    ins = [x2, pos2, w["n1"], w["w_small"], w["w_gdn"], w["w_gg"], w["w_ga"], w["w_gb"], w["w_gt"],
           w["qn"], w["w_uq"], w["kvn"], w["w_uk"], w["w_v"], w["v_one"], w["rope_c"], w["g_col_c"], w["g_row_c"]]
    in_specs = [rows(D_MODEL), rows(1)] + [full(a) for a in ins[2:]]
    out_shape = [
        jax.ShapeDtypeStruct((T, MLA_HEADS * HEAD_PAD), BF16),
        jax.ShapeDtypeStruct((T, MLA_HEADS * HEAD_PAD), BF16),
        jax.ShapeDtypeStruct((T, MLA_HEADS * HEAD_PAD), BF16),
        jax.ShapeDtypeStruct((T, GDN_QKV), BF16),
        jax.ShapeDtypeStruct((T, GDN_WIDTH), BF16),
        jax.ShapeDtypeStruct((T, D_MODEL), BF16),
        jax.ShapeDtypeStruct((T, D_MODEL), BF16),
        jax.ShapeDtypeStruct((T, 4 * GDN_HEADS), F32),
        jax.ShapeDtypeStruct((4 * GDN_HEADS, T), F32),
    ]
    out_specs = [rows(MLA_HEADS * HEAD_PAD), rows(MLA_HEADS * HEAD_PAD), rows(MLA_HEADS * HEAD_PAD), rows(GDN_QKV),
                 rows(GDN_WIDTH), rows(D_MODEL), rows(D_MODEL), rows(4 * GDN_HEADS),
                 pl.BlockSpec((4 * GDN_HEADS, tm), lambda i: (0, i))]
    return pl.pallas_call(
        _in_proj_kernel, grid=grid, in_specs=in_specs, out_specs=out_specs, out_shape=out_shape,
        name="in_proj",
        compiler_params=pltpu.CompilerParams(dimension_semantics=("parallel",),
                                             vmem_limit_bytes=VMEM_LIMIT_BYTES),
    )(*ins)


ATTN_GROUP = 2


def _attn_kernel(q_ref, k_ref, v_ref, o_ref, *, tq):
    S = q_ref.shape[0]

    def q_tile(i, carry):
        rows = pl.ds(pl.multiple_of(i * tq, tq), tq)
        scores = [_dot_nt(q_ref[rows, j * HEAD_PAD:(j + 1) * HEAD_PAD], k_ref[:, j * HEAD_PAD:(j + 1) * HEAD_PAD])
                  for j in range(ATTN_GROUP)]
        outs = []
        for j, s in enumerate(scores):
            p = jnp.exp2((s - jnp.max(s, axis=-1, keepdims=True)).astype(BF16))
            o = _dot(p, v_ref[:, j * HEAD_PAD:(j + 1) * HEAD_PAD])
            outs.append(o[:, :MLA_V] / o[:, MLA_V:MLA_V + 1])
        o_ref[rows, :] = jnp.concatenate(outs, axis=-1).astype(BF16)
        return carry

    lax.fori_loop(0, S // tq, q_tile, 0, unroll=4)


def _attention(q, k, v, tq):
    B, S, _ = q.shape
    group = pl.BlockSpec((None, S, ATTN_GROUP * HEAD_PAD), lambda b, g: (b, 0, g))
    return pl.pallas_call(
        functools.partial(_attn_kernel, tq=tq), grid=(B, MLA_HEADS // ATTN_GROUP),
        in_specs=[group, group, group],
        out_specs=pl.BlockSpec((None, S, ATTN_GROUP * MLA_V), lambda b, g: (b, 0, g)),
        out_shape=jax.ShapeDtypeStruct((B, S, MLA_WIDTH), BF16),
        name="mla_attention",
        compiler_params=pltpu.CompilerParams(dimension_semantics=("parallel", "parallel"),
                                             vmem_limit_bytes=VMEM_LIMIT_BYTES),
    )(q, k, v)


def _bmm(a, b):
    return jnp.einsum("kij,kjl->kil", a, b, preferred_element_type=F32)


def _bmm_nt(a, b):
    return jnp.einsum("kid,kjd->kij", a, b, preferred_element_type=F32)


def _unit_tri_solve(l_mat, rhs, on_diag, eye):
    n1 = jnp.where(on_diag, -l_mat, 0.0)
    l_off = jnp.where(on_diag, 0.0, l_mat)
    n2 = _bmm(n1, n1)
    d = eye + n1
    d = d + _bmm(d, n2)
    n4 = _bmm(n2, n2)
    d = d + _bmm(d, n4)
    n8 = _bmm(n4, n4)
    d = d + _bmm(d, n8)
    db = d.astype(BF16)
    m = _bmm(db, l_off.astype(BF16))
    y0 = _bmm(db, rhs.astype(BF16))
    mb = m.astype(BF16)
    m2 = _bmm(mb, mb)
    z1 = y0 - _bmm(mb, y0.astype(BF16))
    return z1 + _bmm(m2.astype(BF16), z1.astype(BF16))


GDN_HALO = 16


def _gdn_prep_kernel(z_ref, zp_ref, zn_ref, cw_ref, gcol_ref, grow_ref,
                     uf_ref, wf_ref, qef_ref, kdf_ref, qkf_ref,
                     ub_ref, wb_ref, qeb_ref, kdb_ref, qkb_ref, et_ref, pad_s):
    seg = z_ref.shape[0]
    K = seg // CHUNK
    j = pl.program_id(1)
    halo = GDN_HALO

    pad_s[0:halo, :] = jnp.where(j == 0, 0.0, zp_ref[...].astype(F32))
    pad_s[halo:halo + seg, :] = z_ref[...].astype(F32)
    pad_s[halo + seg:, :] = jnp.where(j == pl.num_programs(1) - 1, 0.0, zn_ref[...].astype(F32))

    def conv(c0, l2, mult):
        acc = jnp.zeros((seg, GDN_DK), F32)
        for t in range(GDN_CONV):
            off = halo + t - GDN_CONV // 2
            acc = acc + pad_s[off:off + seg, c0:c0 + GDN_DK] * cw_ref[t:t + 1, c0:c0 + GDN_DK]
        y = _silu(acc)
        if l2:
            y = y * (lax.rsqrt(jnp.sum(y * y, axis=-1, keepdims=True) + EPS) * mult)
        return y.reshape(K, CHUNK, GDN_DK)

    row = lax.broadcasted_iota(jnp.int32, (CHUNK, CHUNK), 0)
    col = lax.broadcasted_iota(jnp.int32, (CHUNK, CHUNK), 1)
    on_diag = (row // SUB) == (col // SUB)
    eye = (row == col).astype(F32)
    gc = gcol_ref[...].reshape(K, CHUNK, 4 * GDN_HEADS)

    systems, rhs_all, stores = [], [], []
    for h in range(GDN_HEADS):
        hs = slice(h * GDN_DK, (h + 1) * GDN_DK)
        q3 = conv(h * GDN_DK, True, GDN_DK ** -0.5)
        k3 = conv(GDN_QK + h * GDN_DK, True, 1.0)
        v3 = conv(2 * GDN_QK + h * GDN_DV, False, 1.0)
        kb = k3.astype(BF16)
        gram = _bmm_nt(kb, kb)
        qk = _bmm_nt(q3.astype(BF16), kb)
        for d, keep_incl, keep_strict, total_row, refs in (
                (0, row >= col, row > col, CHUNK - 1, (uf_ref, wf_ref, qef_ref, kdf_ref, qkf_ref)),
                (1, row <= col, row < col, 0, (ub_ref, wb_ref, qeb_ref, kdb_ref, qkb_ref))):
            u_ref, w_ref, qe_ref, kd_ref, qk_ref = refs
            lane = d * GDN_HEADS + h
            g_col = gc[:, :, lane:lane + 1]
            g_row = grow_ref[:, lane:lane + 1, :]
            beta = gc[:, :, 2 * GDN_HEADS + lane:2 * GDN_HEADS + lane + 1]
            decay = jnp.where(keep_incl, jnp.exp(jnp.where(keep_incl, g_col - g_row, 0.0)), 0.0)
            e_g = jnp.exp(g_col)
            g_total = g_col[:, total_row:total_row + 1, :]
            systems.append(jnp.where(keep_strict, gram * beta * decay, 0.0))
            rhs_all.append(jnp.concatenate([v3 * beta, k3 * (beta * e_g)], axis=-1))
            stores.append((u_ref, w_ref, hs))
            qe_ref[:, hs] = (q3 * e_g).astype(BF16).reshape(seg, GDN_DK)
            kd_ref[:, hs] = (k3 * jnp.exp(g_total - g_col)).astype(BF16).reshape(seg, GDN_DK)
            qk_ref[h] = (qk * decay).astype(BF16).reshape(seg, CHUNK)
            et_ref[:, lane:lane + 1, :] = jnp.broadcast_to(jnp.exp(g_total), (K, 1, GDN_DV))

    sol = _unit_tri_solve(jnp.concatenate(systems, axis=0), jnp.concatenate(rhs_all, axis=0), on_diag, eye)
    for i, (u_ref, w_ref, hs) in enumerate(stores):
        part = sol[i * K:(i + 1) * K]
        u_ref[:, hs] = part[:, :, :GDN_DV].astype(BF16).reshape(seg, GDN_DV)
        w_ref[:, hs] = part[:, :, GDN_DV:].astype(BF16).reshape(seg, GDN_DK)


def _gdn_scan_kernel(uf_ref, wf_ref, qef_ref, kdf_ref, qkf_ref, etf_ref,
                     ub_ref, wb_ref, qeb_ref, kdb_ref, qkb_ref, etb_ref,
                     of_ref, ob_ref, st_ref):
    seg = uf_ref.shape[0]
    K = seg // CHUNK

    @pl.when(pl.program_id(1) == 0)
    def _():
        st_ref[...] = jnp.zeros(st_ref.shape, F32)

    dirs = ((uf_ref, wf_ref, qef_ref, kdf_ref, qkf_ref, etf_ref, of_ref),
            (ub_ref, wb_ref, qeb_ref, kdb_ref, qkb_ref, etb_ref, ob_ref))

    def step(t, carry):
        chains = []
        for d, (u_ref, w_ref, qe_ref, kd_ref, qk_ref, et_ref, o_ref) in enumerate(dirs):
            c = t if d == 0 else K - 1 - t
            rows = pl.ds(pl.multiple_of(c * CHUNK, CHUNK), CHUNK)
            for h in range(GDN_HEADS):
                chains.append((d * GDN_HEADS + h, h, c, rows, slice(h * GDN_DK, (h + 1) * GDN_DK),
                               u_ref, w_ref, qe_ref, kd_ref, qk_ref, et_ref, o_ref))
        sb = [st_ref[ch[0]].astype(BF16) for ch in chains]
        v_new = [u_ref[rows, hs] - _dot(w_ref[rows, hs], s)
                 for s, (_, _, _, rows, hs, u_ref, w_ref, *_) in zip(sb, chains)]
        o_inter = [_dot(qe_ref[rows, hs], s)
                   for s, (_, _, _, rows, hs, _, _, qe_ref, *_) in zip(sb, chains)]
        vb = [v.astype(BF16) for v in v_new]
        for v, oi, (lane, h, c, rows, hs, _, _, _, kd_ref, qk_ref, et_ref, o_ref) in zip(vb, o_inter, chains):
            o_ref[rows, hs] = (oi + _dot(qk_ref[h, rows, :], v)).astype(BF16)
            st_ref[lane] = st_ref[lane] * et_ref[c, lane:lane + 1, :] + _dot_tn(kd_ref[rows, hs], v)
        return carry

    lax.fori_loop(0, K, step, 0, unroll=True)


def _gdn(zg, conv_w, gcol, grow, seg):
    B, S, _ = zg.shape
    H = GDN_HEADS
    nseg = S // seg
    K = seg // CHUNK
    hb = seg // GDN_HALO
    last_hb = S // GDN_HALO - 1

    wide = lambda dt: jax.ShapeDtypeStruct((B, S, H * GDN_DK), dt)
    qk_shape = jax.ShapeDtypeStruct((B, H, S, CHUNK), BF16)
    et_shape = jax.ShapeDtypeStruct((B, S // CHUNK, 2 * H, GDN_DV), F32)
    dir_shapes = [wide(BF16), wide(BF16), wide(BF16), wide(BF16), qk_shape]
    seg_spec = lambda imap: pl.BlockSpec((None, seg, H * GDN_DK), imap)
    qk_spec = lambda imap: pl.BlockSpec((None, H, seg, CHUNK), imap)
    et_spec = lambda imap: pl.BlockSpec((None, K, 2 * H, GDN_DV), imap)

    here3 = lambda b, j: (b, j, 0)
    here4 = lambda b, j: (b, 0, j, 0)
    dir_specs = [seg_spec(here3)] * 4 + [qk_spec(here4)]
    prep = pl.pallas_call(
        _gdn_prep_kernel, grid=(B, nseg),
        in_specs=[pl.BlockSpec((None, seg, GDN_QKV), here3),
                  pl.BlockSpec((None, GDN_HALO, GDN_QKV), lambda b, j: (b, jnp.maximum(j * hb - 1, 0), 0)),
                  pl.BlockSpec((None, GDN_HALO, GDN_QKV), lambda b, j: (b, jnp.minimum((j + 1) * hb, last_hb), 0)),
                  pl.BlockSpec((GDN_CONV, GDN_QKV), lambda b, j: (0, 0)),
                  pl.BlockSpec((None, seg, 4 * H), here3),
                  pl.BlockSpec((None, K, 4 * H, CHUNK), lambda b, j: (b, j, 0, 0))],
        out_specs=dir_specs + dir_specs + [et_spec(lambda b, j: (b, j, 0, 0))],
        out_shape=dir_shapes + dir_shapes + [et_shape],
        scratch_shapes=[pltpu.VMEM((seg + 2 * GDN_HALO, GDN_QKV), F32)],
        name="gdn_prep",
        compiler_params=pltpu.CompilerParams(dimension_semantics=("parallel", "parallel"),
                                             vmem_limit_bytes=VMEM_LIMIT_BYTES),
    )(zg, zg, zg, conv_w, gcol, grow)
    fwd, bwd, et = prep[:5], prep[5:10], prep[10]

    back3 = lambda b, j: (b, nseg - 1 - j, 0)
    back4 = lambda b, j: (b, 0, nseg - 1 - j, 0)
    return pl.pallas_call(
        _gdn_scan_kernel, grid=(B, nseg),
        in_specs=[seg_spec(here3)] * 4 + [qk_spec(here4), et_spec(lambda b, j: (b, j, 0, 0))]
                 + [seg_spec(back3)] * 4 + [qk_spec(back4), et_spec(lambda b, j: (b, nseg - 1 - j, 0, 0))],
        out_specs=[seg_spec(here3), seg_spec(back3)],
        out_shape=[wide(BF16), wide(BF16)],
        scratch_shapes=[pltpu.VMEM((2 * H, GDN_DK, GDN_DV), F32)],
        name="gdn_scan",
        compiler_params=pltpu.CompilerParams(dimension_semantics=("parallel", "arbitrary"),
                                             vmem_limit_bytes=VMEM_LIMIT_BYTES),
    )(*fwd, et, *bwd, et)


def _out_kernel(x_ref, oa_ref, of_ref, ob_ref, gg_ref, sa_ref, sb_ref, gnw_ref, wpa_ref, wpb_ref, wo_ref,
                n2_ref, w1_ref, w2_ref, nf_ref, y_ref):
    parts = []
    for h in range(GDN_HEADS):
        hs = slice(h * GDN_DV, (h + 1) * GDN_DV)
        o = of_ref[:, hs].astype(F32) + ob_ref[:, hs].astype(F32)
        parts.append((_rms(o, gnw_ref[...]) * gg_ref[:, hs].astype(F32)).astype(BF16))
    o_b = jnp.concatenate(parts, axis=-1)

    merged = (sa_ref[...].astype(F32) * _dot(oa_ref[...], wpa_ref[...])
              + sb_ref[...].astype(F32) * _dot(o_b, wpb_ref[...]))
    x1 = x_ref[...] + _dot(merged.astype(BF16), wo_ref[...])
    h2 = _rms(x1, n2_ref[...]).astype(BF16)
    a = jnp.maximum(_dot(h2, w1_ref[...]), 0.0)
    x2 = x1 + _dot((a * a).astype(BF16), w2_ref[...])
    y_ref[...] = _rms(x2, nf_ref[...])


def _out_proj(x2, oa, o_f, o_b, gg, sa, sb, w, tm):
    T = x2.shape[0]

    def full(a):
        return pl.BlockSpec(a.shape, lambda i: (0,) * a.ndim, pipeline_mode=pl.Buffered(1))

    def rows(width):
        return pl.BlockSpec((tm, width), lambda i: (i, 0))

    consts = [w["gnw"], w["w_pa"], w["w_pb"], w["w_o"], w["n2"], w["w_1"], w["w_2"], w["nf"]]
    return pl.pallas_call(
        _out_kernel, grid=(T // tm,),
        in_specs=[rows(D_MODEL), rows(MLA_WIDTH), rows(GDN_WIDTH), rows(GDN_WIDTH), rows(GDN_WIDTH),
                  rows(D_MODEL), rows(D_MODEL)] + [full(a) for a in consts],
        out_specs=rows(D_MODEL),
        out_shape=jax.ShapeDtypeStruct((T, D_MODEL), F32),
        name="merge_mlp",
        compiler_params=pltpu.CompilerParams(dimension_semantics=("parallel",),
                                             vmem_limit_bytes=VMEM_LIMIT_BYTES),
    )(x2, oa, o_f, o_b, gg, sa, sb, *consts)


def _prepare_weights(norm1_w, w_in, q_norm_w, w_uq, kv_norm_w, w_ukv, a_log_f, dt_bias_f, a_log_b,
                     dt_bias_b, gdn_norm_w, w_proj_a, w_proj_b, w_out, norm2_w, w_ff1, w_ff2, final_norm_w):
    H = GDN_HEADS
    o_kv = MLA_Q_LORA
    o_kr = o_kv + MLA_KV_LORA
    o_gdn = o_kr + MLA_ROPE
    o_gab = o_gdn + GDN_QKV
    o_gg = o_gab + 4 * H
    o_ga = o_gg + GDN_WIDTH
    o_gb = o_ga + D_MODEL
    zeros = lambda n: jnp.zeros((D_MODEL, n), F32)
    w_kr = jnp.concatenate([zeros(MLA_NOPE), w_in[:, o_kr:o_gdn], zeros(HEAD_PAD - MLA_NOPE - MLA_ROPE)], 1)
    w_gab = w_in[:, o_gab:o_gg]
    w_small = jnp.concatenate([w_in[:, :o_kr], w_kr, w_gab, zeros(HEAD_PAD - 4 * H)], 1)

    dq = MLA_NOPE + MLA_ROPE
    w_uq_p = jnp.pad(w_uq.reshape(MLA_Q_LORA, MLA_HEADS, dq), ((0, 0), (0, 0), (0, HEAD_PAD - dq)))
    w_ukv_r = w_ukv.reshape(MLA_KV_LORA, MLA_HEADS, MLA_NOPE + MLA_V)
    w_uk_p = jnp.pad(w_ukv_r[:, :, :MLA_NOPE], ((0, 0), (0, 0), (0, HEAD_PAD - MLA_NOPE)))
    w_v_p = jnp.pad(w_ukv_r[:, :, MLA_NOPE:], ((0, 0), (0, 0), (0, HEAD_PAD - MLA_V)))
    v_one = jnp.tile((jnp.arange(HEAD_PAD) == MLA_V).astype(F32), MLA_HEADS).reshape(1, MLA_HEADS * HEAD_PAD)

    inv_freq = 1.0 / (ROPE_THETA ** (jnp.arange(0, MLA_ROPE, 2, dtype=F32) / MLA_ROPE))
    z16 = jnp.zeros((ROPE_HALF,), F32)
    one16 = jnp.ones((ROPE_HALF,), F32)
    lane = lambda lo, hi: jnp.concatenate([jnp.zeros((MLA_NOPE,), F32), lo, hi,
                                           jnp.zeros((HEAD_PAD - MLA_NOPE - MLA_ROPE,), F32)])
    rope_c = jnp.stack([lane(inv_freq, inv_freq), lane(-one16, z16), lane(z16, one16)])

    bias = jnp.concatenate([dt_bias_f, dt_bias_b, jnp.zeros((2 * H,), F32)])
    neg_a = jnp.concatenate([-jnp.exp(a_log_f), -jnp.exp(a_log_b), jnp.zeros((2 * H,), F32)])
    g_row_c = jnp.stack([bias, neg_a], axis=1)
    g_col_c = jnp.pad(jnp.stack([bias, neg_a]), ((0, 0), (0, HEAD_PAD - 4 * H)))

    return {
        "n1": norm1_w.reshape(1, D_MODEL),
        "w_small": w_small.astype(BF16),
        "w_gdn": w_in[:, o_gdn:o_gab].astype(BF16),
        "w_gg": w_in[:, o_gg:o_ga].astype(BF16),
        "w_ga": w_in[:, o_ga:o_gb].astype(BF16),
        "w_gb": w_in[:, o_gb:].astype(BF16),
        "w_gt": w_gab.T.astype(BF16),
        "qn": q_norm_w.reshape(1, MLA_Q_LORA),
        "w_uq": w_uq_p.reshape(MLA_Q_LORA, MLA_HEADS * HEAD_PAD).astype(BF16),
        "kvn": kv_norm_w.reshape(1, MLA_KV_LORA),
        "w_uk": w_uk_p.reshape(MLA_KV_LORA, MLA_HEADS * HEAD_PAD).astype(BF16),
        "w_v": w_v_p.reshape(MLA_KV_LORA, MLA_HEADS * HEAD_PAD).astype(BF16),
        "v_one": v_one,
        "rope_c": rope_c,
        "g_col_c": g_col_c,
        "g_row_c": g_row_c,
        "gnw": gdn_norm_w.reshape(1, GDN_DV),
        "w_pa": w_proj_a.astype(BF16),
        "w_pb": w_proj_b.astype(BF16),
        "w_o": w_out.astype(BF16),
        "n2": norm2_w.reshape(1, D_MODEL),
        "w_1": w_ff1.astype(BF16),
        "w_2": w_ff2.astype(BF16),
        "nf": final_norm_w.reshape(1, D_MODEL),
    }


def _layer(x, positions, conv_w, w):
    B, S, _ = x.shape
    T = B * S
    tm = min(512, S)
    x2 = x.reshape(T, D_MODEL)
    q, k, v, zg, gg, sa, sb, gcol, grow = _in_proj(x2, positions.reshape(T, 1), w, tm)

    o_a = _attention(q.reshape(B, S, -1), k.reshape(B, S, -1), v.reshape(B, S, -1), min(512, S))

    grow_c = grow.reshape(4 * GDN_HEADS, B, S // CHUNK, CHUNK).transpose(1, 2, 0, 3)
    o_f, o_b = _gdn(zg.reshape(B, S, GDN_QKV), conv_w, gcol.reshape(B, S, 4 * GDN_HEADS), grow_c, min(512, S))

    y = _out_proj(x2, o_a.reshape(T, MLA_WIDTH), o_f.reshape(T, GDN_WIDTH), o_b.reshape(T, GDN_WIDTH),
                  gg, sa, sb, w, tm)
    return y.reshape(B, S, D_MODEL)


def kernel(x, positions, norm1_w, w_in, q_norm_w, w_uq, kv_norm_w, w_ukv, conv_w, a_log_f, dt_bias_f,
           a_log_b, dt_bias_b, gdn_norm_w, w_proj_a, w_proj_b, w_out, norm2_w, w_ff1, w_ff2,
           final_norm_w):
    assert norm1_w.shape[0] == 1, "single-layer block"
    w = _prepare_weights(norm1_w[0], w_in[0], q_norm_w[0], w_uq[0], kv_norm_w[0], w_ukv[0], a_log_f[0],
                         dt_bias_f[0], a_log_b[0], dt_bias_b[0], gdn_norm_w[0], w_proj_a[0], w_proj_b[0],
                         w_out[0], norm2_w[0], w_ff1[0], w_ff2[0], final_norm_w)
    return _layer(x, positions, conv_w[0], w)
```

```python
import functools
import math

import jax
import jax.numpy as jnp
from jax import lax
from jax.experimental import pallas as pl
from jax.experimental.pallas import tpu as pltpu

F32 = jnp.float32
BF16 = jnp.bfloat16

D_MODEL = 1024
MLA_HEADS = 8
MLA_Q_LORA = 256
MLA_KV_LORA = 128
MLA_NOPE = 64
MLA_ROPE = 32
MLA_V = 64
ROPE_THETA = 10000.0
GDN_HEADS = 4
GDN_DK = 128
GDN_DV = 128
GDN_CONV = 5
CHUNK = 64
SUB = 16
D_FF = 4 * D_MODEL
EPS = 1e-6

MLA_WIDTH = MLA_HEADS * MLA_V
GDN_QK = GDN_HEADS * GDN_DK
GDN_WIDTH = GDN_HEADS * GDN_DV
GDN_QKV = 2 * GDN_QK + GDN_WIDTH
HEAD_PAD = 128
ROPE_HALF = MLA_ROPE // 2

VMEM_LIMIT_BYTES = 56 * 1024 * 1024


def _dot(a, b):
    return jnp.dot(a, b, preferred_element_type=F32)


def _dot_nt(a, b):
    return lax.dot_general(a, b, (((1,), (1,)), ((), ())), preferred_element_type=F32)


def _dot_tn(a, b):
    return lax.dot_general(a, b, (((0,), (0,)), ((), ())), preferred_element_type=F32)


def _sigmoid(x):
    return 1.0 / (1.0 + jnp.exp(-x))


def _silu(x):
    return x * _sigmoid(x)


def _softplus(x):
    return jnp.maximum(x, 0.0) + jnp.log(1.0 + jnp.exp(-jnp.abs(x)))


def _rms(x, w):
    return x * lax.rsqrt(jnp.mean(x * x, axis=-1, keepdims=True) + EPS) * w


def _chunk_cumsum(x, axis, reverse):
    n = x.shape[axis]
    pos = lax.broadcasted_iota(jnp.int32, x.shape, axis) % CHUNK
    s = 1
    while s < CHUNK:
        if reverse:
            shifted = pltpu.roll(x, n - s, axis)
            x = x + jnp.where(pos < CHUNK - s, shifted, 0.0)
        else:
            shifted = pltpu.roll(x, s, axis)
            x = x + jnp.where(pos >= s, shifted, 0.0)
        s *= 2
    return x


def _decay_beta(z, bias, neg_a, axis):
    idx = lax.broadcasted_iota(jnp.int32, z.shape, 1 - axis)
    g = neg_a * _softplus(z + bias)
    g_f = _chunk_cumsum(jnp.where(idx < GDN_HEADS, g, 0.0), axis, reverse=False)
    g_b = _chunk_cumsum(jnp.where((idx >= GDN_HEADS) & (idx < 2 * GDN_HEADS), g, 0.0), axis, reverse=True)
    return jnp.where(idx < GDN_HEADS, g_f, jnp.where(idx < 2 * GDN_HEADS, g_b, _sigmoid(z)))


def _rope(t, cos, sin_lo, sin_hi):
    return (t * cos + pltpu.roll(t, HEAD_PAD - ROPE_HALF, 1) * sin_lo
            + pltpu.roll(t, ROPE_HALF, 1) * sin_hi)


def _in_proj_kernel(x_ref, pos_ref, n1_ref, wsm_ref, wgdn_ref, wgg_ref, wga_ref, wgb_ref, wgt_ref,
                    qn_ref, wuq_ref, kvn_ref, wukv_ref, vone_ref, ropec_ref, gcc_ref, gcr_ref,
                    q_out, k_out, v_out, zg_out, gg_out, sa_out, sb_out, gcol_out, grow_out):
    hb = _rms(x_ref[...], n1_ref[...]).astype(BF16)

    zs = _dot(hb, wsm_ref[...])
    c_q = zs[:, :MLA_Q_LORA]
    c_kv = zs[:, MLA_Q_LORA:MLA_Q_LORA + MLA_KV_LORA]
    k_r = zs[:, 384:512]
    z_gab = zs[:, 512:640]

    wide = [(sa_out, wga_ref, c0, _sigmoid) for c0 in range(0, D_MODEL, 512)]
    wide += [(sb_out, wgb_ref, c0, _sigmoid) for c0 in range(0, D_MODEL, 512)]
    wide += [(gg_out, wgg_ref, 0, _silu)]
    wide += [(zg_out, wgdn_ref, c0, None) for c0 in range(0, GDN_QKV, 512)]
    wide = iter(wide)

    def wide_piece():
        out_ref, w_ref, c0, act = next(wide)
        y = _dot(hb, w_ref[:, c0:c0 + 512])
        out_ref[:, c0:c0 + 512] = (y if act is None else act(y)).astype(BF16)

    wide_piece()
    ang = pos_ref[...].astype(F32) * ropec_ref[0:1, :]
    cos = jnp.cos(ang)
    wide_piece()
    sin = jnp.sin(ang)
    sin_lo = sin * ropec_ref[1:2, :]
    sin_hi = sin * ropec_ref[2:3, :]
    wide_piece()

    qf = _dot(_rms(c_q, qn_ref[...]).astype(BF16), wuq_ref[...])
    scale = math.log2(math.e) / math.sqrt(MLA_NOPE + MLA_ROPE)
    for h in range(MLA_HEADS):
        sl = slice(h * HEAD_PAD, (h + 1) * HEAD_PAD)
        q_out[:, sl] = (_rope(qf[:, sl], cos, sin_lo, sin_hi) * scale).astype(BF16)
        if h % 4 == 3:
            wide_piece()

    ckv = _rms(c_kv, kvn_ref[...]).astype(BF16)
    kv = _dot(ckv, wukv_ref[...])
    kr = _rope(k_r, cos, sin_lo, sin_hi)
    low = lax.broadcasted_iota(jnp.int32, (1, HEAD_PAD), 1) < MLA_NOPE
    for h in range(MLA_HEADS):
        sl = slice(h * HEAD_PAD, (h + 1) * HEAD_PAD)
        k_out[:, sl] = jnp.where(low, kv[:, sl], kr).astype(BF16)
        v_out[:, sl] = jnp.where(low, pltpu.roll(kv[:, sl], MLA_V, 1), vone_ref[...]).astype(BF16)
    wide_piece()

    col = _decay_beta(z_gab, gcc_ref[0:1, :], gcc_ref[1:2, :], axis=0)
    gcol_out[...] = col[:, :4 * GDN_HEADS]
    wide_piece()
    z_t = _dot_nt(wgt_ref[...], hb)
    grow_out[...] = _decay_beta(z_t, gcr_ref[:, 0:1], gcr_ref[:, 1:2], axis=1)
    wide_piece()
    assert next(wide, None) is None


def _in_proj(x2, pos2, w, tm):
    T = x2.shape[0]
    grid = (T // tm,)

    def full(a):
        return pl.BlockSpec(a.shape, lambda i: (0,) * a.ndim, pipeline_mode=pl.Buffered(1))

    def rows(width):
        return pl.BlockSpec((tm, width), lambda i: (i, 0))

    ins = [x2, pos2, w["n1"], w["w_small"], w["w_gdn"], w["w_gg"], w["w_ga"], w["w_gb"], w["w_gt"],
           w["qn"], w["w_uq"], w["kvn"], w["w_ukv"], w["v_one"], w["rope_c"], w["g_col_c"], w["g_row_c"]]
    in_specs = [rows(D_MODEL), rows(1)] + [full(a) for a in ins[2:]]
    out_shape = [
        jax.ShapeDtypeStruct((T, MLA_HEADS * HEAD_PAD), BF16),
        jax.ShapeDtypeStruct((T, MLA_HEADS * HEAD_PAD), BF16),
        jax.ShapeDtypeStruct((T, MLA_HEADS * HEAD_PAD), BF16),
        jax.ShapeDtypeStruct((T, GDN_QKV), BF16),
        jax.ShapeDtypeStruct((T, GDN_WIDTH), BF16),
        jax.ShapeDtypeStruct((T, D_MODEL), BF16),
        jax.ShapeDtypeStruct((T, D_MODEL), BF16),
        jax.ShapeDtypeStruct((T, 4 * GDN_HEADS), F32),
        jax.ShapeDtypeStruct((4 * GDN_HEADS, T), F32),
    ]
    out_specs = [rows(MLA_HEADS * HEAD_PAD), rows(MLA_HEADS * HEAD_PAD), rows(MLA_HEADS * HEAD_PAD), rows(GDN_QKV),
                 rows(GDN_WIDTH), rows(D_MODEL), rows(D_MODEL), rows(4 * GDN_HEADS),
                 pl.BlockSpec((4 * GDN_HEADS, tm), lambda i: (0, i))]
    return pl.pallas_call(
        _in_proj_kernel, grid=grid, in_specs=in_specs, out_specs=out_specs, out_shape=out_shape,
        name="in_proj",
        compiler_params=pltpu.CompilerParams(dimension_semantics=("parallel",),
                                             vmem_limit_bytes=VMEM_LIMIT_BYTES),
    )(*ins)


ATTN_GROUP = 2
ATTN_Q_TILES = 8
GDN_HALO = 16


def _attn_kernel(q_ref, k_ref, v_ref, *rest, tq):
    z_refs, zp_refs, zn_refs, cw_refs = rest[0:3], rest[3:6], rest[6:9], rest[9:12]
    o_ref, conv_out_refs, pad_s = rest[12], rest[13:16], rest[16]
    R = q_ref.shape[0]
    it = pl.program_id(2)
    halo = GDN_HALO

    for s, (z_ref, zp_ref, zn_ref) in enumerate(zip(z_refs, zp_refs, zn_refs)):
        pad_s[s, 0:halo, :] = jnp.where(it == 0, 0.0, zp_ref[...].astype(F32))
        pad_s[s, halo:halo + R, :] = z_ref[...].astype(F32)
        pad_s[s, halo + R:, :] = jnp.where(it == pl.num_programs(2) - 1, 0.0, zn_ref[...].astype(F32))

    stream_norm = ((True, GDN_DK ** -0.5), (True, 1.0), (False, 1.0))

    for i in range(R // tq):
        rows = slice(i * tq, (i + 1) * tq)
        for s, (cw_ref, out_ref, (l2, mult)) in enumerate(zip(cw_refs, conv_out_refs, stream_norm)):
            acc = jnp.zeros((tq, GDN_DK), F32)
            for t in range(GDN_CONV):
                off = halo + i * tq + t - GDN_CONV // 2
                acc = acc + pad_s[s, off:off + tq, :] * cw_ref[t:t + 1, :]
            y = _silu(acc)
            if l2:
                y = y * (lax.rsqrt(jnp.sum(y * y, axis=-1, keepdims=True) + EPS) * mult)
            out_ref[rows, :] = y.astype(BF16)

        scores = [_dot_nt(q_ref[rows, j * HEAD_PAD:(j + 1) * HEAD_PAD], k_ref[:, j * HEAD_PAD:(j + 1) * HEAD_PAD])
                  for j in range(ATTN_GROUP)]

        outs = []
        for j, s in enumerate(scores):
            p = jnp.exp2((s - jnp.max(s, axis=-1, keepdims=True)).astype(BF16))
            o = _dot(p, v_ref[:, j * HEAD_PAD:(j + 1) * HEAD_PAD])
            outs.append(o[:, :MLA_V] / o[:, MLA_V:MLA_V + 1])
        o_ref[rows, :] = jnp.concatenate(outs, axis=-1).astype(BF16)


def _attention(q, k, v, zg, conv_w, tq):
    B, S, _ = q.shape
    H = GDN_HEADS
    assert MLA_HEADS // ATTN_GROUP == H
    R = min(ATTN_Q_TILES * tq, S)
    hb = R // GDN_HALO
    last_hb = S // GDN_HALO - 1
    kv = pl.BlockSpec((None, S, ATTN_GROUP * HEAD_PAD), lambda b, g, r: (b, 0, g))
    z_main = [pl.BlockSpec((None, R, GDN_DK), lambda b, g, r, s=s: (b, r, s * H + g)) for s in range(3)]
    z_prev = [pl.BlockSpec((None, GDN_HALO, GDN_DK), lambda b, g, r, s=s: (b, jnp.maximum(r * hb - 1, 0), s * H + g))
              for s in range(3)]
    z_next = [pl.BlockSpec((None, GDN_HALO, GDN_DK),
                           lambda b, g, r, s=s: (b, jnp.minimum((r + 1) * hb, last_hb), s * H + g)) for s in range(3)]
    cw = [pl.BlockSpec((GDN_CONV, GDN_DK), lambda b, g, r, s=s: (0, s * H + g)) for s in range(3)]
    slab_out = pl.BlockSpec((None, R, GDN_DK), lambda b, g, r: (b, r, g))
    slab_shape = jax.ShapeDtypeStruct((B, S, H * GDN_DK), BF16)
    return pl.pallas_call(
        functools.partial(_attn_kernel, tq=tq), grid=(B, MLA_HEADS // ATTN_GROUP, S // R),
        in_specs=[pl.BlockSpec((None, R, ATTN_GROUP * HEAD_PAD), lambda b, g, r: (b, r, g)), kv, kv]
                 + z_main + z_prev + z_next + cw,
        out_specs=[pl.BlockSpec((None, R, ATTN_GROUP * MLA_V), lambda b, g, r: (b, r, g)),
                   slab_out, slab_out, slab_out],
        out_shape=[jax.ShapeDtypeStruct((B, S, MLA_WIDTH), BF16), slab_shape, slab_shape, slab_shape],
        scratch_shapes=[pltpu.VMEM((3, R + 2 * GDN_HALO, GDN_DK), F32)],
        name="mla_attention",
        compiler_params=pltpu.CompilerParams(dimension_semantics=("parallel", "parallel", "parallel"),
                                             vmem_limit_bytes=VMEM_LIMIT_BYTES),
    )(q, k, v, *([zg] * 9), *([conv_w] * 3))


def _bmm(a, b):
    return jnp.einsum("kij,kjl->kil", a, b, preferred_element_type=F32)


def _bmm_nt(a, b):
    return jnp.einsum("kid,kjd->kij", a, b, preferred_element_type=F32)


def _unit_tri_solve(l_mat, rhs, on_diag, eye):
    n1 = jnp.where(on_diag, -l_mat, 0.0)
    l_off = jnp.where(on_diag, 0.0, l_mat)
    n2 = _bmm(n1, n1)
    d = eye + n1
    d = d + _bmm(d, n2)
    n4 = _bmm(n2, n2)
    d = d + _bmm(d, n4)
    n8 = _bmm(n4, n4)
    d = d + _bmm(d, n8)
    db = d.astype(BF16)
    m = _bmm(db, l_off.astype(BF16))
    y0 = _bmm(db, rhs.astype(BF16))
    mb = m.astype(BF16)
    m2 = _bmm(mb, mb)
    z1 = y0 - _bmm(mb, y0.astype(BF16))
    return z1 + _bmm(m2.astype(BF16), z1.astype(BF16))


def _gdn_prep_kernel(qc_ref, kc_ref, vc_ref, gcol_ref, grow_ref,
                     uf_ref, wf_ref, qef_ref, kdf_ref, qkf_ref,
                     ub_ref, wb_ref, qeb_ref, kdb_ref, qkb_ref, et_ref):
    seg = qc_ref.shape[0]
    K = seg // CHUNK

    row = lax.broadcasted_iota(jnp.int32, (CHUNK, CHUNK), 0)
    col = lax.broadcasted_iota(jnp.int32, (CHUNK, CHUNK), 1)
    on_diag = (row // SUB) == (col // SUB)
    eye = (row == col).astype(F32)
    gc = gcol_ref[...].reshape(K, CHUNK, 4 * GDN_HEADS)

    systems, rhs_all, stores = [], [], []
    for h in range(GDN_HEADS):
        hs = slice(h * GDN_DK, (h + 1) * GDN_DK)
        qb = qc_ref[:, hs].reshape(K, CHUNK, GDN_DK)
        kb = kc_ref[:, hs].reshape(K, CHUNK, GDN_DK)
        q3, k3, v3 = qb.astype(F32), kb.astype(F32), vc_ref[:, hs].astype(F32).reshape(K, CHUNK, GDN_DV)
        gram = _bmm_nt(kb, kb)
        qk = _bmm_nt(qb, kb)
        for d, keep_incl, keep_strict, total_row, refs in (
                (0, row >= col, row > col, CHUNK - 1, (uf_ref, wf_ref, qef_ref, kdf_ref, qkf_ref)),
                (1, row <= col, row < col, 0, (ub_ref, wb_ref, qeb_ref, kdb_ref, qkb_ref))):
            u_ref, w_ref, qe_ref, kd_ref, qk_ref = refs
            lane = d * GDN_HEADS + h
            g_col = gc[:, :, lane:lane + 1]
            g_row = grow_ref[:, lane:lane + 1, :]
            beta = gc[:, :, 2 * GDN_HEADS + lane:2 * GDN_HEADS + lane + 1]
            decay = jnp.where(keep_incl, jnp.exp(jnp.where(keep_incl, g_col - g_row, 0.0)), 0.0)
            e_g = jnp.exp(g_col)
            g_total = g_col[:, total_row:total_row + 1, :]
            systems.append(jnp.where(keep_strict, gram * beta * decay, 0.0))
            rhs_all.append(jnp.concatenate([v3 * beta, k3 * (beta * e_g)], axis=-1))
            stores.append((u_ref, w_ref, hs))
            qe_ref[:, hs] = (q3 * e_g).astype(BF16).reshape(seg, GDN_DK)
            kd_ref[:, hs] = (k3 * jnp.exp(g_total - g_col)).astype(BF16).reshape(seg, GDN_DK)
            qk_ref[h] = (qk * decay).astype(BF16).reshape(seg, CHUNK)
            et_ref[:, lane:lane + 1, :] = jnp.broadcast_to(jnp.exp(g_total), (K, 1, GDN_DV))

    sol = _unit_tri_solve(jnp.concatenate(systems, axis=0), jnp.concatenate(rhs_all, axis=0), on_diag, eye)
    for i, (u_ref, w_ref, hs) in enumerate(stores):
        part = sol[i * K:(i + 1) * K]
        u_ref[:, hs] = part[:, :, :GDN_DV].astype(BF16).reshape(seg, GDN_DV)
        w_ref[:, hs] = part[:, :, GDN_DV:].astype(BF16).reshape(seg, GDN_DK)


def _gdn_scan_kernel(uf_ref, wf_ref, qef_ref, kdf_ref, qkf_ref, etf_ref,
                     ub_ref, wb_ref, qeb_ref, kdb_ref, qkb_ref, etb_ref,
                     of_ref, ob_ref, st_ref):
    seg = uf_ref.shape[0]
    K = seg // CHUNK

    @pl.when(pl.program_id(1) == 0)
    def _():
        st_ref[...] = jnp.zeros(st_ref.shape, F32)

    dirs = ((uf_ref, wf_ref, qef_ref, kdf_ref, qkf_ref, etf_ref, of_ref),
            (ub_ref, wb_ref, qeb_ref, kdb_ref, qkb_ref, etb_ref, ob_ref))

    def step(t, carry):
        chains = []
        for d, (u_ref, w_ref, qe_ref, kd_ref, qk_ref, et_ref, o_ref) in enumerate(dirs):
            c = t if d == 0 else K - 1 - t
            rows = pl.ds(pl.multiple_of(c * CHUNK, CHUNK), CHUNK)
            for h in range(GDN_HEADS):
                chains.append((d * GDN_HEADS + h, h, c, rows, slice(h * GDN_DK, (h + 1) * GDN_DK),
                               u_ref, w_ref, qe_ref, kd_ref, qk_ref, et_ref, o_ref))
        sb = [st_ref[ch[0]].astype(BF16) for ch in chains]
        v_new = [u_ref[rows, hs] - _dot(w_ref[rows, hs], s)
                 for s, (_, _, _, rows, hs, u_ref, w_ref, *_) in zip(sb, chains)]
        o_inter = [_dot(qe_ref[rows, hs], s)
                   for s, (_, _, _, rows, hs, _, _, qe_ref, *_) in zip(sb, chains)]
        vb = [v.astype(BF16) for v in v_new]
        for v, oi, (lane, h, c, rows, hs, _, _, _, kd_ref, qk_ref, et_ref, o_ref) in zip(vb, o_inter, chains):
            o_ref[rows, hs] = (oi + _dot(qk_ref[h, rows, :], v)).astype(BF16)
            st_ref[lane] = st_ref[lane] * et_ref[c, lane:lane + 1, :] + _dot_tn(kd_ref[rows, hs], v)
        return carry

    lax.fori_loop(0, K, step, 0, unroll=True)


def _gdn(qc, kc, vc, gcol, grow, prep_seg, scan_seg):
    B, S, _ = qc.shape
    H = GDN_HEADS

    wide = lambda dt: jax.ShapeDtypeStruct((B, S, H * GDN_DK), dt)
    qk_shape = jax.ShapeDtypeStruct((B, H, S, CHUNK), BF16)
    et_shape = jax.ShapeDtypeStruct((B, S // CHUNK, 2 * H, GDN_DV), F32)
    dir_shapes = [wide(BF16), wide(BF16), wide(BF16), wide(BF16), qk_shape]
    here3 = lambda b, j: (b, j, 0)
    here4 = lambda b, j: (b, 0, j, 0)

    seg, K = prep_seg, prep_seg // CHUNK
    seg_spec = lambda imap: pl.BlockSpec((None, seg, H * GDN_DK), imap)
    qk_spec = lambda imap: pl.BlockSpec((None, H, seg, CHUNK), imap)
    et_spec = lambda imap: pl.BlockSpec((None, K, 2 * H, GDN_DV), imap)
    dir_specs = [seg_spec(here3)] * 4 + [qk_spec(here4)]
    prep = pl.pallas_call(
        _gdn_prep_kernel, grid=(B, S // seg),
        in_specs=[seg_spec(here3), seg_spec(here3), seg_spec(here3),
                  pl.BlockSpec((None, seg, 4 * H), here3),
                  pl.BlockSpec((None, K, 4 * H, CHUNK), lambda b, j: (b, j, 0, 0))],
        out_specs=dir_specs + dir_specs + [et_spec(lambda b, j: (b, j, 0, 0))],
        out_shape=dir_shapes + dir_shapes + [et_shape],
        name="gdn_prep",
        compiler_params=pltpu.CompilerParams(dimension_semantics=("parallel", "parallel"),
                                             vmem_limit_bytes=VMEM_LIMIT_BYTES),
    )(qc, kc, vc, gcol, grow)
    fwd, bwd, et = prep[:5], prep[5:10], prep[10]

    seg, K = scan_seg, scan_seg // CHUNK
    nseg = S // seg
    seg_spec = lambda imap: pl.BlockSpec((None, seg, H * GDN_DK), imap)
    qk_spec = lambda imap: pl.BlockSpec((None, H, seg, CHUNK), imap)
    et_spec = lambda imap: pl.BlockSpec((None, K, 2 * H, GDN_DV), imap)
    back3 = lambda b, j: (b, nseg - 1 - j, 0)
    back4 = lambda b, j: (b, 0, nseg - 1 - j, 0)
    return pl.pallas_call(
        _gdn_scan_kernel, grid=(B, nseg),
        in_specs=[seg_spec(here3)] * 4 + [qk_spec(here4), et_spec(lambda b, j: (b, j, 0, 0))]
                 + [seg_spec(back3)] * 4 + [qk_spec(back4), et_spec(lambda b, j: (b, nseg - 1 - j, 0, 0))],
        out_specs=[seg_spec(here3), seg_spec(back3)],
        out_shape=[wide(BF16), wide(BF16)],
        scratch_shapes=[pltpu.VMEM((2 * H, GDN_DK, GDN_DV), F32)],
        name="gdn_scan",
        compiler_params=pltpu.CompilerParams(dimension_semantics=("parallel", "arbitrary"),
                                             vmem_limit_bytes=VMEM_LIMIT_BYTES),
    )(*fwd, et, *bwd, et)


def _out_kernel(x_ref, oa_ref, of_ref, ob_ref, gg_ref, sa_ref, sb_ref, gnw_ref, wpa_ref, wpb_ref, wo_ref,
                n2_ref, w1_ref, w2_ref, nf_ref, y_ref):
    parts = []
    for h in range(GDN_HEADS):
        hs = slice(h * GDN_DV, (h + 1) * GDN_DV)
        o = of_ref[:, hs].astype(F32) + ob_ref[:, hs].astype(F32)
        parts.append((_rms(o, gnw_ref[...]) * gg_ref[:, hs].astype(F32)).astype(BF16))
    o_b = jnp.concatenate(parts, axis=-1)

    merged = (sa_ref[...].astype(F32) * _dot(oa_ref[...], wpa_ref[...])
              + sb_ref[...].astype(F32) * _dot(o_b, wpb_ref[...]))
    x1 = x_ref[...] + _dot(merged.astype(BF16), wo_ref[...])
    h2 = _rms(x1, n2_ref[...]).astype(BF16)
    a = jnp.maximum(_dot(h2, w1_ref[...]), 0.0)
    x2 = x1 + _dot((a * a).astype(BF16), w2_ref[...])
    y_ref[...] = _rms(x2, nf_ref[...])


def _out_proj(x2, oa, o_f, o_b, gg, sa, sb, w, tm):
    T = x2.shape[0]

    def full(a):
        return pl.BlockSpec(a.shape, lambda i: (0,) * a.ndim, pipeline_mode=pl.Buffered(1))

    def rows(width):
        return pl.BlockSpec((tm, width), lambda i: (i, 0))

    consts = [w["gnw"], w["w_pa"], w["w_pb"], w["w_o"], w["n2"], w["w_1"], w["w_2"], w["nf"]]
    return pl.pallas_call(
        _out_kernel, grid=(T // tm,),
        in_specs=[rows(D_MODEL), rows(MLA_WIDTH), rows(GDN_WIDTH), rows(GDN_WIDTH), rows(GDN_WIDTH),
                  rows(D_MODEL), rows(D_MODEL)] + [full(a) for a in consts],
        out_specs=rows(D_MODEL),
        out_shape=jax.ShapeDtypeStruct((T, D_MODEL), F32),
        name="merge_mlp",
        compiler_params=pltpu.CompilerParams(dimension_semantics=("parallel",),
                                             vmem_limit_bytes=VMEM_LIMIT_BYTES),
    )(x2, oa, o_f, o_b, gg, sa, sb, *consts)


def _prepare_weights(norm1_w, w_in, q_norm_w, w_uq, kv_norm_w, w_ukv, a_log_f, dt_bias_f, a_log_b,
                     dt_bias_b, gdn_norm_w, w_proj_a, w_proj_b, w_out, norm2_w, w_ff1, w_ff2, final_norm_w):
    H = GDN_HEADS
    o_kv = MLA_Q_LORA
    o_kr = o_kv + MLA_KV_LORA
    o_gdn = o_kr + MLA_ROPE
    o_gab = o_gdn + GDN_QKV
    o_gg = o_gab + 4 * H
    o_ga = o_gg + GDN_WIDTH
    o_gb = o_ga + D_MODEL
    zeros = lambda n: jnp.zeros((D_MODEL, n), F32)
    w_kr = jnp.concatenate([zeros(MLA_NOPE), w_in[:, o_kr:o_gdn], zeros(HEAD_PAD - MLA_NOPE - MLA_ROPE)], 1)
    w_gab = w_in[:, o_gab:o_gg]
    w_small = jnp.concatenate([w_in[:, :o_kr], w_kr, w_gab, zeros(HEAD_PAD - 4 * H)], 1)

    dq = MLA_NOPE + MLA_ROPE
    w_uq_p = jnp.pad(w_uq.reshape(MLA_Q_LORA, MLA_HEADS, dq), ((0, 0), (0, 0), (0, HEAD_PAD - dq)))
    assert MLA_NOPE + MLA_V == HEAD_PAD and MLA_NOPE == MLA_V
    v_one = (jnp.arange(HEAD_PAD) == MLA_V).astype(F32).reshape(1, HEAD_PAD)

    inv_freq = 1.0 / (ROPE_THETA ** (jnp.arange(0, MLA_ROPE, 2, dtype=F32) / MLA_ROPE))
    z16 = jnp.zeros((ROPE_HALF,), F32)
    one16 = jnp.ones((ROPE_HALF,), F32)
    lane = lambda lo, hi: jnp.concatenate([jnp.zeros((MLA_NOPE,), F32), lo, hi,
                                           jnp.zeros((HEAD_PAD - MLA_NOPE - MLA_ROPE,), F32)])
    rope_c = jnp.stack([lane(inv_freq, inv_freq), lane(-one16, z16), lane(z16, one16)])

    bias = jnp.concatenate([dt_bias_f, dt_bias_b, jnp.zeros((2 * H,), F32)])
    neg_a = jnp.concatenate([-jnp.exp(a_log_f), -jnp.exp(a_log_b), jnp.zeros((2 * H,), F32)])
    g_row_c = jnp.stack([bias, neg_a], axis=1)
    g_col_c = jnp.pad(jnp.stack([bias, neg_a]), ((0, 0), (0, HEAD_PAD - 4 * H)))

    return {
        "n1": norm1_w.reshape(1, D_MODEL),
        "w_small": w_small.astype(BF16),
        "w_gdn": w_in[:, o_gdn:o_gab].astype(BF16),
        "w_gg": w_in[:, o_gg:o_ga].astype(BF16),
        "w_ga": w_in[:, o_ga:o_gb].astype(BF16),
        "w_gb": w_in[:, o_gb:].astype(BF16),
        "w_gt": w_gab.T.astype(BF16),
        "qn": q_norm_w.reshape(1, MLA_Q_LORA),
        "w_uq": w_uq_p.reshape(MLA_Q_LORA, MLA_HEADS * HEAD_PAD).astype(BF16),
        "kvn": kv_norm_w.reshape(1, MLA_KV_LORA),
        "w_ukv": w_ukv.astype(BF16),
        "v_one": v_one,
        "rope_c": rope_c,
        "g_col_c": g_col_c,
        "g_row_c": g_row_c,
        "gnw": gdn_norm_w.reshape(1, GDN_DV),
        "w_pa": w_proj_a.astype(BF16),
        "w_pb": w_proj_b.astype(BF16),
        "w_o": w_out.astype(BF16),
        "n2": norm2_w.reshape(1, D_MODEL),
        "w_1": w_ff1.astype(BF16),
        "w_2": w_ff2.astype(BF16),
        "nf": final_norm_w.reshape(1, D_MODEL),
    }


def _tile_sizes(S):
    return dict(
        tm=min(512, S),
        tq=min(512, S),
        prep_seg=min(512, S),
        scan_seg=min(1024, S),
    )


def _layer(x, positions, conv_w, w):
    B, S, _ = x.shape
    T = B * S
    ts = _tile_sizes(S)
    tm = ts["tm"]
    x2 = x.reshape(T, D_MODEL)
    q, k, v, zg, gg, sa, sb, gcol, grow = _in_proj(x2, positions.reshape(T, 1), w, tm)

    o_a, qc, kc, vc = _attention(q.reshape(B, S, -1), k.reshape(B, S, -1), v.reshape(B, S, -1),
                                 zg.reshape(B, S, GDN_QKV), conv_w, ts["tq"])

    grow_c = grow.reshape(4 * GDN_HEADS, B, S // CHUNK, CHUNK).transpose(1, 2, 0, 3)
    o_f, o_b = _gdn(qc, kc, vc, gcol.reshape(B, S, 4 * GDN_HEADS), grow_c, ts["prep_seg"], ts["scan_seg"])

    y = _out_proj(x2, o_a.reshape(T, MLA_WIDTH), o_f.reshape(T, GDN_WIDTH), o_b.reshape(T, GDN_WIDTH),
                  gg, sa, sb, w, tm)
    return y.reshape(B, S, D_MODEL)


def kernel(x, positions, norm1_w, w_in, q_norm_w, w_uq, kv_norm_w, w_ukv, conv_w, a_log_f, dt_bias_f,
           a_log_b, dt_bias_b, gdn_norm_w, w_proj_a, w_proj_b, w_out, norm2_w, w_ff1, w_ff2,
           final_norm_w):
    assert norm1_w.shape[0] == 1, "single-layer block"
    w = _prepare_weights(norm1_w[0], w_in[0], q_norm_w[0], w_uq[0], kv_norm_w[0], w_ukv[0], a_log_f[0],
                         dt_bias_f[0], a_log_b[0], dt_bias_b[0], gdn_norm_w[0], w_proj_a[0], w_proj_b[0],
                         w_out[0], norm2_w[0], w_ff1[0], w_ff2[0], final_norm_w)
    return _layer(x, positions, conv_w[0], w)
```

```python
import functools
import math

import jax
import jax.numpy as jnp
from jax import lax
from jax.experimental import pallas as pl
from jax.experimental.pallas import tpu as pltpu

F32 = jnp.float32
BF16 = jnp.bfloat16

D_MODEL = 1024
MLA_HEADS = 8
MLA_Q_LORA = 256
MLA_KV_LORA = 128
MLA_NOPE = 64
MLA_ROPE = 32
MLA_V = 64
ROPE_THETA = 10000.0
GDN_HEADS = 4
GDN_DK = 128
GDN_DV = 128
GDN_CONV = 5
CHUNK = 64
SUB = 16
D_FF = 4 * D_MODEL
EPS = 1e-6

MLA_WIDTH = MLA_HEADS * MLA_V
GDN_QK = GDN_HEADS * GDN_DK
GDN_WIDTH = GDN_HEADS * GDN_DV
GDN_QKV = 2 * GDN_QK + GDN_WIDTH
HEAD_PAD = 128
ROPE_HALF = MLA_ROPE // 2

VMEM_LIMIT_BYTES = 56 * 1024 * 1024


def _dot(a, b):
    return jnp.dot(a, b, preferred_element_type=F32)


def _dot_nt(a, b):
    return lax.dot_general(a, b, (((1,), (1,)), ((), ())), preferred_element_type=F32)


def _dot_tn(a, b):
    return lax.dot_general(a, b, (((0,), (0,)), ((), ())), preferred_element_type=F32)


def _sigmoid(x):
    return 1.0 / (1.0 + jnp.exp(-x))


def _silu(x):
    return x * _sigmoid(x)


def _softplus(x):
    return jnp.maximum(x, 0.0) + jnp.log(1.0 + jnp.exp(-jnp.abs(x)))


def _rms(x, w):
    return x * lax.rsqrt(jnp.mean(x * x, axis=-1, keepdims=True) + EPS) * w


def _chunk_cumsum(x, axis, reverse):
    n = x.shape[axis]
    pos = lax.broadcasted_iota(jnp.int32, x.shape, axis) % CHUNK
    s = 1
    while s < CHUNK:
        if reverse:
            shifted = pltpu.roll(x, n - s, axis)
            x = x + jnp.where(pos < CHUNK - s, shifted, 0.0)
        else:
            shifted = pltpu.roll(x, s, axis)
            x = x + jnp.where(pos >= s, shifted, 0.0)
        s *= 2
    return x


def _decay_beta(z, bias, neg_a, axis):
    idx = lax.broadcasted_iota(jnp.int32, z.shape, 1 - axis)
    g = neg_a * _softplus(z + bias)
    g_f = _chunk_cumsum(jnp.where(idx < GDN_HEADS, g, 0.0), axis, reverse=False)
    g_b = _chunk_cumsum(jnp.where((idx >= GDN_HEADS) & (idx < 2 * GDN_HEADS), g, 0.0), axis, reverse=True)
    return jnp.where(idx < GDN_HEADS, g_f, jnp.where(idx < 2 * GDN_HEADS, g_b, _sigmoid(z)))


def _rope(t, cos, sin_lo, sin_hi):
    return (t * cos + pltpu.roll(t, HEAD_PAD - ROPE_HALF, 1) * sin_lo
            + pltpu.roll(t, ROPE_HALF, 1) * sin_hi)


def _in_proj_kernel(x_ref, pos_ref, n1_ref, wsm_ref, wgdn_ref, wgg_ref, wga_ref, wgb_ref, wgt_ref,
                    qn_ref, wuq_ref, kvn_ref, wukv_ref, vone_ref, ropec_ref, gcc_ref, gcr_ref,
                    q_out, k_out, v_out, zg_out, gg_out, sa_out, sb_out, gcol_out, grow_out):
    hb = _rms(x_ref[...], n1_ref[...]).astype(BF16)

    zs = _dot(hb, wsm_ref[...])
    c_q = zs[:, :MLA_Q_LORA]
    c_kv = zs[:, MLA_Q_LORA:MLA_Q_LORA + MLA_KV_LORA]
    k_r = zs[:, 384:512]
    z_gab = zs[:, 512:640]

    wide = [(sa_out, wga_ref, c0, _sigmoid) for c0 in range(0, D_MODEL, 512)]
    wide += [(sb_out, wgb_ref, c0, _sigmoid) for c0 in range(0, D_MODEL, 512)]
    wide += [(gg_out, wgg_ref, 0, _silu)]
    wide += [(zg_out, wgdn_ref, c0, None) for c0 in range(0, GDN_QKV, 512)]
    wide = iter(wide)

    def wide_piece():
        out_ref, w_ref, c0, act = next(wide)
        y = _dot(hb, w_ref[:, c0:c0 + 512])
        out_ref[:, c0:c0 + 512] = (y if act is None else act(y)).astype(BF16)

    wide_piece()
    ang = pos_ref[...].astype(F32) * ropec_ref[0:1, :]
    cos = jnp.cos(ang)
    wide_piece()
    sin = jnp.sin(ang)
    sin_lo = sin * ropec_ref[1:2, :]
    sin_hi = sin * ropec_ref[2:3, :]
    wide_piece()

    qf = _dot(_rms(c_q, qn_ref[...]).astype(BF16), wuq_ref[...])
    scale = math.log2(math.e) / math.sqrt(MLA_NOPE + MLA_ROPE)
    for h in range(MLA_HEADS):
        sl = slice(h * HEAD_PAD, (h + 1) * HEAD_PAD)
        q_out[:, sl] = (_rope(qf[:, sl], cos, sin_lo, sin_hi) * scale).astype(BF16)
        if h % 4 == 3:
            wide_piece()

    ckv = _rms(c_kv, kvn_ref[...]).astype(BF16)
    kv = _dot(ckv, wukv_ref[...])
    kr = _rope(k_r, cos, sin_lo, sin_hi)
    low = lax.broadcasted_iota(jnp.int32, (1, HEAD_PAD), 1) < MLA_NOPE
    for h in range(MLA_HEADS):
        sl = slice(h * HEAD_PAD, (h + 1) * HEAD_PAD)
        k_out[:, sl] = jnp.where(low, kv[:, sl], kr).astype(BF16)
        v_out[:, sl] = jnp.where(low, pltpu.roll(kv[:, sl], MLA_V, 1), vone_ref[...]).astype(BF16)
    wide_piece()

    col = _decay_beta(z_gab, gcc_ref[0:1, :], gcc_ref[1:2, :], axis=0)
    gcol_out[...] = col[:, :4 * GDN_HEADS]
    wide_piece()
    z_t = _dot_nt(wgt_ref[...], hb)
    grow_out[...] = _decay_beta(z_t, gcr_ref[:, 0:1], gcr_ref[:, 1:2], axis=1)
    wide_piece()
    assert next(wide, None) is None


def _in_proj(x2, pos2, w, tm):
    T = x2.shape[0]
    grid = (T // tm,)

    def full(a):
        return pl.BlockSpec(a.shape, lambda i: (0,) * a.ndim, pipeline_mode=pl.Buffered(1))

    def rows(width):
        return pl.BlockSpec((tm, width), lambda i: (i, 0))

    ins = [x2, pos2, w["n1"], w["w_small"], w["w_gdn"], w["w_gg"], w["w_ga"], w["w_gb"], w["w_gt"],
           w["qn"], w["w_uq"], w["kvn"], w["w_ukv"], w["v_one"], w["rope_c"], w["g_col_c"], w["g_row_c"]]
    in_specs = [rows(D_MODEL), rows(1)] + [full(a) for a in ins[2:]]
    out_shape = [
        jax.ShapeDtypeStruct((T, MLA_HEADS * HEAD_PAD), BF16),
        jax.ShapeDtypeStruct((T, MLA_HEADS * HEAD_PAD), BF16),
        jax.ShapeDtypeStruct((T, MLA_HEADS * HEAD_PAD), BF16),
        jax.ShapeDtypeStruct((T, GDN_QKV), BF16),
        jax.ShapeDtypeStruct((T, GDN_WIDTH), BF16),
        jax.ShapeDtypeStruct((T, D_MODEL), BF16),
        jax.ShapeDtypeStruct((T, D_MODEL), BF16),
        jax.ShapeDtypeStruct((T, 4 * GDN_HEADS), F32),
        jax.ShapeDtypeStruct((4 * GDN_HEADS, T), F32),
    ]
    out_specs = [rows(MLA_HEADS * HEAD_PAD), rows(MLA_HEADS * HEAD_PAD), rows(MLA_HEADS * HEAD_PAD), rows(GDN_QKV),
                 rows(GDN_WIDTH), rows(D_MODEL), rows(D_MODEL), rows(4 * GDN_HEADS),
                 pl.BlockSpec((4 * GDN_HEADS, tm), lambda i: (0, i))]
    return pl.pallas_call(
        _in_proj_kernel, grid=grid, in_specs=in_specs, out_specs=out_specs, out_shape=out_shape,
        name="in_proj",
        compiler_params=pltpu.CompilerParams(dimension_semantics=("parallel",),
                                             vmem_limit_bytes=VMEM_LIMIT_BYTES),
    )(*ins)


ATTN_GROUP = 2
ATTN_Q_TILES = 2
GDN_HALO = 16


def _attn_kernel(q_ref, k_ref, v_ref, *rest, tq):
    z_refs, zp_refs, zn_refs, cw_refs = rest[0:3], rest[3:6], rest[6:9], rest[9:12]
    o_ref, conv_out_refs, pad_s = rest[12], rest[13:16], rest[16]
    R = q_ref.shape[0]
    it = pl.program_id(2)
    halo = GDN_HALO

    for s, (z_ref, zp_ref, zn_ref) in enumerate(zip(z_refs, zp_refs, zn_refs)):
        pad_s[s, 0:halo, :] = jnp.where(it == 0, 0.0, zp_ref[...].astype(F32))
        pad_s[s, halo:halo + R, :] = z_ref[...].astype(F32)
        pad_s[s, halo + R:, :] = jnp.where(it == pl.num_programs(2) - 1, 0.0, zn_ref[...].astype(F32))

    stream_norm = ((True, GDN_DK ** -0.5), (True, 1.0), (False, 1.0))

    for i in range(R // tq):
        rows = slice(i * tq, (i + 1) * tq)
        for s, (cw_ref, out_ref, (l2, mult)) in enumerate(zip(cw_refs, conv_out_refs, stream_norm)):
            acc = jnp.zeros((tq, GDN_DK), F32)
            for t in range(GDN_CONV):
                off = halo + i * tq + t - GDN_CONV // 2
                acc = acc + pad_s[s, off:off + tq, :] * cw_ref[t:t + 1, :]
            y = _silu(acc)
            if l2:
                y = y * (lax.rsqrt(jnp.sum(y * y, axis=-1, keepdims=True) + EPS) * mult)
            out_ref[rows, :] = y.astype(BF16)

        scores = [_dot_nt(q_ref[rows, j * HEAD_PAD:(j + 1) * HEAD_PAD], k_ref[:, j * HEAD_PAD:(j + 1) * HEAD_PAD])
                  for j in range(ATTN_GROUP)]

        outs = []
        for j, s in enumerate(scores):
            p = jnp.exp2((s - jnp.max(s, axis=-1, keepdims=True)).astype(BF16))
            o = _dot(p, v_ref[:, j * HEAD_PAD:(j + 1) * HEAD_PAD])
            outs.append(o[:, :MLA_V] / o[:, MLA_V:MLA_V + 1])
        o_ref[rows, :] = jnp.concatenate(outs, axis=-1).astype(BF16)


def _attention(q, k, v, zg, conv_w, tq):
    B, S, _ = q.shape
    H = GDN_HEADS
    assert MLA_HEADS // ATTN_GROUP == H
    R = min(ATTN_Q_TILES * tq, S)
    hb = R // GDN_HALO
    last_hb = S // GDN_HALO - 1
    kv = pl.BlockSpec((None, S, ATTN_GROUP * HEAD_PAD), lambda b, g, r: (b, 0, g))
    z_main = [pl.BlockSpec((None, R, GDN_DK), lambda b, g, r, s=s: (b, r, s * H + g)) for s in range(3)]
    z_prev = [pl.BlockSpec((None, GDN_HALO, GDN_DK), lambda b, g, r, s=s: (b, jnp.maximum(r * hb - 1, 0), s * H + g))
              for s in range(3)]
    z_next = [pl.BlockSpec((None, GDN_HALO, GDN_DK),
                           lambda b, g, r, s=s: (b, jnp.minimum((r + 1) * hb, last_hb), s * H + g)) for s in range(3)]
    cw = [pl.BlockSpec((GDN_CONV, GDN_DK), lambda b, g, r, s=s: (0, s * H + g)) for s in range(3)]
    slab_out = pl.BlockSpec((None, R, GDN_DK), lambda b, g, r: (b, r, g))
    slab_shape = jax.ShapeDtypeStruct((B, S, H * GDN_DK), BF16)
    return pl.pallas_call(
        functools.partial(_attn_kernel, tq=tq), grid=(B, MLA_HEADS // ATTN_GROUP, S // R),
        in_specs=[pl.BlockSpec((None, R, ATTN_GROUP * HEAD_PAD), lambda b, g, r: (b, r, g)), kv, kv]
                 + z_main + z_prev + z_next + cw,
        out_specs=[pl.BlockSpec((None, R, ATTN_GROUP * MLA_V), lambda b, g, r: (b, r, g)),
                   slab_out, slab_out, slab_out],
        out_shape=[jax.ShapeDtypeStruct((B, S, MLA_WIDTH), BF16), slab_shape, slab_shape, slab_shape],
        scratch_shapes=[pltpu.VMEM((3, R + 2 * GDN_HALO, GDN_DK), F32)],
        name="mla_attention",
        compiler_params=pltpu.CompilerParams(dimension_semantics=("parallel", "parallel", "parallel"),
                                             vmem_limit_bytes=VMEM_LIMIT_BYTES),
    )(q, k, v, *([zg] * 9), *([conv_w] * 3))


def _bmm(a, b):
    return jnp.einsum("kij,kjl->kil", a, b, preferred_element_type=F32)


def _bmm_nt(a, b):
    return jnp.einsum("kid,kjd->kij", a, b, preferred_element_type=F32)


def _unit_tri_solve(l_mat, rhs, on_diag, eye):
    n1 = jnp.where(on_diag, -l_mat, 0.0)
    l_off = jnp.where(on_diag, 0.0, l_mat)
    n2 = _bmm(n1, n1)
    d = eye + n1
    d = d + _bmm(d, n2)
    n4 = _bmm(n2, n2)
    d = d + _bmm(d, n4)
    n8 = _bmm(n4, n4)
    d = d + _bmm(d, n8)
    db = d.astype(BF16)
    m = _bmm(db, l_off.astype(BF16))
    y0 = _bmm(db, rhs.astype(BF16))
    mb = m.astype(BF16)
    m2 = _bmm(mb, mb)
    z1 = y0 - _bmm(mb, y0.astype(BF16))
    return z1 + _bmm(m2.astype(BF16), z1.astype(BF16))


def _gdn_prep_kernel(qc_ref, kc_ref, vc_ref, gcol_ref, grow_ref,
                     uf_ref, wf_ref, qef_ref, kdf_ref, qkf_ref,
                     ub_ref, wb_ref, qeb_ref, kdb_ref, qkb_ref, et_ref):
    seg = qc_ref.shape[0]
    K = seg // CHUNK

    row = lax.broadcasted_iota(jnp.int32, (CHUNK, CHUNK), 0)
    col = lax.broadcasted_iota(jnp.int32, (CHUNK, CHUNK), 1)
    on_diag = (row // SUB) == (col // SUB)
    eye = (row == col).astype(F32)
    gc = gcol_ref[...].reshape(K, CHUNK, 4 * GDN_HEADS)

    systems, rhs_all, stores = [], [], []
    for h in range(GDN_HEADS):
        hs = slice(h * GDN_DK, (h + 1) * GDN_DK)
        qb = qc_ref[:, hs].reshape(K, CHUNK, GDN_DK)
        kb = kc_ref[:, hs].reshape(K, CHUNK, GDN_DK)
        q3, k3, v3 = qb.astype(F32), kb.astype(F32), vc_ref[:, hs].astype(F32).reshape(K, CHUNK, GDN_DV)
        gram = _bmm_nt(kb, kb)
        qk = _bmm_nt(qb, kb)
        for d, keep_incl, keep_strict, total_row, refs in (
                (0, row >= col, row > col, CHUNK - 1, (uf_ref, wf_ref, qef_ref, kdf_ref, qkf_ref)),
                (1, row <= col, row < col, 0, (ub_ref, wb_ref, qeb_ref, kdb_ref, qkb_ref))):
            u_ref, w_ref, qe_ref, kd_ref, qk_ref = refs
            lane = d * GDN_HEADS + h
            g_col = gc[:, :, lane:lane + 1]
            g_row = grow_ref[:, lane:lane + 1, :]
            beta = gc[:, :, 2 * GDN_HEADS + lane:2 * GDN_HEADS + lane + 1]
            decay = jnp.where(keep_incl, jnp.exp(jnp.where(keep_incl, g_col - g_row, 0.0)), 0.0)
            e_g = jnp.exp(g_col)
            g_total = g_col[:, total_row:total_row + 1, :]
            systems.append(jnp.where(keep_strict, gram * beta * decay, 0.0))
            rhs_all.append(jnp.concatenate([v3 * beta, k3 * (beta * e_g)], axis=-1))
            stores.append((u_ref, w_ref, hs))
            qe_ref[:, hs] = (q3 * e_g).astype(BF16).reshape(seg, GDN_DK)
            kd_ref[:, hs] = (k3 * jnp.exp(g_total - g_col)).astype(BF16).reshape(seg, GDN_DK)
            qk_ref[h] = (qk * decay).astype(BF16).reshape(seg, CHUNK)
            et_ref[:, lane:lane + 1, :] = jnp.broadcast_to(jnp.exp(g_total), (K, 1, GDN_DV))

    sol = _unit_tri_solve(jnp.concatenate(systems, axis=0), jnp.concatenate(rhs_all, axis=0), on_diag, eye)
    for i, (u_ref, w_ref, hs) in enumerate(stores):
        part = sol[i * K:(i + 1) * K]
        u_ref[:, hs] = part[:, :, :GDN_DV].astype(BF16).reshape(seg, GDN_DV)
        w_ref[:, hs] = part[:, :, GDN_DV:].astype(BF16).reshape(seg, GDN_DK)


def _gdn_scan_kernel(uf_ref, wf_ref, qef_ref, kdf_ref, qkf_ref, etf_ref,
                     ub_ref, wb_ref, qeb_ref, kdb_ref, qkb_ref, etb_ref,
                     of_ref, ob_ref, st_ref):
    seg = uf_ref.shape[0]
    K = seg // CHUNK

    @pl.when(pl.program_id(1) == 0)
    def _():
        st_ref[...] = jnp.zeros(st_ref.shape, F32)

    dirs = ((uf_ref, wf_ref, qef_ref, kdf_ref, qkf_ref, etf_ref, of_ref),
            (ub_ref, wb_ref, qeb_ref, kdb_ref, qkb_ref, etb_ref, ob_ref))

    def step(t, carry):
        chains = []
        for d, (u_ref, w_ref, qe_ref, kd_ref, qk_ref, et_ref, o_ref) in enumerate(dirs):
            c = t if d == 0 else K - 1 - t
            rows = pl.ds(pl.multiple_of(c * CHUNK, CHUNK), CHUNK)
            for h in range(GDN_HEADS):
                chains.append((d * GDN_HEADS + h, h, c, rows, slice(h * GDN_DK, (h + 1) * GDN_DK),
                               u_ref, w_ref, qe_ref, kd_ref, qk_ref, et_ref, o_ref))
        sb = [st_ref[ch[0]].astype(BF16) for ch in chains]
        v_new = [u_ref[rows, hs] - _dot(w_ref[rows, hs], s)
                 for s, (_, _, _, rows, hs, u_ref, w_ref, *_) in zip(sb, chains)]
        o_inter = [_dot(qe_ref[rows, hs], s)
                   for s, (_, _, _, rows, hs, _, _, qe_ref, *_) in zip(sb, chains)]
        vb = [v.astype(BF16) for v in v_new]
        for v, oi, (lane, h, c, rows, hs, _, _, _, kd_ref, qk_ref, et_ref, o_ref) in zip(vb, o_inter, chains):
            o_ref[rows, hs] = (oi + _dot(qk_ref[h, rows, :], v)).astype(BF16)
            st_ref[lane] = st_ref[lane] * et_ref[c, lane:lane + 1, :] + _dot_tn(kd_ref[rows, hs], v)
        return carry

    lax.fori_loop(0, K, step, 0, unroll=True)


def _gdn(qc, kc, vc, gcol, grow, prep_seg, scan_seg):
    B, S, _ = qc.shape
    H = GDN_HEADS

    wide = lambda dt: jax.ShapeDtypeStruct((B, S, H * GDN_DK), dt)
    qk_shape = jax.ShapeDtypeStruct((B, H, S, CHUNK), BF16)
    et_shape = jax.ShapeDtypeStruct((B, S // CHUNK, 2 * H, GDN_DV), F32)
    dir_shapes = [wide(BF16), wide(BF16), wide(BF16), wide(BF16), qk_shape]
    here3 = lambda b, j: (b, j, 0)
    here4 = lambda b, j: (b, 0, j, 0)

    seg, K = prep_seg, prep_seg // CHUNK
    seg_spec = lambda imap: pl.BlockSpec((None, seg, H * GDN_DK), imap)
    qk_spec = lambda imap: pl.BlockSpec((None, H, seg, CHUNK), imap)
    et_spec = lambda imap: pl.BlockSpec((None, K, 2 * H, GDN_DV), imap)
    dir_specs = [seg_spec(here3)] * 4 + [qk_spec(here4)]
    prep = pl.pallas_call(
        _gdn_prep_kernel, grid=(B, S // seg),
        in_specs=[seg_spec(here3), seg_spec(here3), seg_spec(here3),
                  pl.BlockSpec((None, seg, 4 * H), here3),
                  pl.BlockSpec((None, K, 4 * H, CHUNK), lambda b, j: (b, j, 0, 0))],
        out_specs=dir_specs + dir_specs + [et_spec(lambda b, j: (b, j, 0, 0))],
        out_shape=dir_shapes + dir_shapes + [et_shape],
        name="gdn_prep",
        compiler_params=pltpu.CompilerParams(dimension_semantics=("parallel", "parallel"),
                                             vmem_limit_bytes=VMEM_LIMIT_BYTES),
    )(qc, kc, vc, gcol, grow)
    fwd, bwd, et = prep[:5], prep[5:10], prep[10]

    seg, K = scan_seg, scan_seg // CHUNK
    nseg = S // seg
    seg_spec = lambda imap: pl.BlockSpec((None, seg, H * GDN_DK), imap)
    qk_spec = lambda imap: pl.BlockSpec((None, H, seg, CHUNK), imap)
    et_spec = lambda imap: pl.BlockSpec((None, K, 2 * H, GDN_DV), imap)
    back3 = lambda b, j: (b, nseg - 1 - j, 0)
    back4 = lambda b, j: (b, 0, nseg - 1 - j, 0)
    return pl.pallas_call(
        _gdn_scan_kernel, grid=(B, nseg),
        in_specs=[seg_spec(here3)] * 4 + [qk_spec(here4), et_spec(lambda b, j: (b, j, 0, 0))]
                 + [seg_spec(back3)] * 4 + [qk_spec(back4), et_spec(lambda b, j: (b, nseg - 1 - j, 0, 0))],
        out_specs=[seg_spec(here3), seg_spec(back3)],
        out_shape=[wide(BF16), wide(BF16)],
        scratch_shapes=[pltpu.VMEM((2 * H, GDN_DK, GDN_DV), F32)],
        name="gdn_scan",
        compiler_params=pltpu.CompilerParams(dimension_semantics=("parallel", "arbitrary"),
                                             vmem_limit_bytes=VMEM_LIMIT_BYTES),
    )(*fwd, et, *bwd, et)


def _out_kernel(x_ref, oa_ref, of_ref, ob_ref, gg_ref, sa_ref, sb_ref, gnw_ref, wpa_ref, wpb_ref, wo_ref,
                n2_ref, w1_ref, w2_ref, nf_ref, y_ref):
    parts = []
    for h in range(GDN_HEADS):
        hs = slice(h * GDN_DV, (h + 1) * GDN_DV)
        o = of_ref[:, hs].astype(F32) + ob_ref[:, hs].astype(F32)
        parts.append((_rms(o, gnw_ref[...]) * gg_ref[:, hs].astype(F32)).astype(BF16))
    o_b = jnp.concatenate(parts, axis=-1)

    merged = (sa_ref[...].astype(F32) * _dot(oa_ref[...], wpa_ref[...])
              + sb_ref[...].astype(F32) * _dot(o_b, wpb_ref[...]))
    x1 = x_ref[...] + _dot(merged.astype(BF16), wo_ref[...])
    h2 = _rms(x1, n2_ref[...]).astype(BF16)
    a = jnp.maximum(_dot(h2, w1_ref[...]), 0.0)
    x2 = x1 + _dot((a * a).astype(BF16), w2_ref[...])
    y_ref[...] = _rms(x2, nf_ref[...])


def _out_proj(x2, oa, o_f, o_b, gg, sa, sb, w, tm):
    T = x2.shape[0]

    def full(a):
        return pl.BlockSpec(a.shape, lambda i: (0,) * a.ndim, pipeline_mode=pl.Buffered(1))

    def rows(width):
        return pl.BlockSpec((tm, width), lambda i: (i, 0))

    consts = [w["gnw"], w["w_pa"], w["w_pb"], w["w_o"], w["n2"], w["w_1"], w["w_2"], w["nf"]]
    return pl.pallas_call(
        _out_kernel, grid=(T // tm,),
        in_specs=[rows(D_MODEL), rows(MLA_WIDTH), rows(GDN_WIDTH), rows(GDN_WIDTH), rows(GDN_WIDTH),
                  rows(D_MODEL), rows(D_MODEL)] + [full(a) for a in consts],
        out_specs=rows(D_MODEL),
        out_shape=jax.ShapeDtypeStruct((T, D_MODEL), F32),
        name="merge_mlp",
        compiler_params=pltpu.CompilerParams(dimension_semantics=("parallel",),
                                             vmem_limit_bytes=VMEM_LIMIT_BYTES),
    )(x2, oa, o_f, o_b, gg, sa, sb, *consts)


def _prepare_weights(norm1_w, w_in, q_norm_w, w_uq, kv_norm_w, w_ukv, a_log_f, dt_bias_f, a_log_b,
                     dt_bias_b, gdn_norm_w, w_proj_a, w_proj_b, w_out, norm2_w, w_ff1, w_ff2, final_norm_w):
    H = GDN_HEADS
    o_kv = MLA_Q_LORA
    o_kr = o_kv + MLA_KV_LORA
    o_gdn = o_kr + MLA_ROPE
    o_gab = o_gdn + GDN_QKV
    o_gg = o_gab + 4 * H
    o_ga = o_gg + GDN_WIDTH
    o_gb = o_ga + D_MODEL
    zeros = lambda n: jnp.zeros((D_MODEL, n), F32)
    w_kr = jnp.concatenate([zeros(MLA_NOPE), w_in[:, o_kr:o_gdn], zeros(HEAD_PAD - MLA_NOPE - MLA_ROPE)], 1)
    w_gab = w_in[:, o_gab:o_gg]
    w_small = jnp.concatenate([w_in[:, :o_kr], w_kr, w_gab, zeros(HEAD_PAD - 4 * H)], 1)

    dq = MLA_NOPE + MLA_ROPE
    w_uq_p = jnp.pad(w_uq.reshape(MLA_Q_LORA, MLA_HEADS, dq), ((0, 0), (0, 0), (0, HEAD_PAD - dq)))
    assert MLA_NOPE + MLA_V == HEAD_PAD and MLA_NOPE == MLA_V
    v_one = (jnp.arange(HEAD_PAD) == MLA_V).astype(F32).reshape(1, HEAD_PAD)

    inv_freq = 1.0 / (ROPE_THETA ** (jnp.arange(0, MLA_ROPE, 2, dtype=F32) / MLA_ROPE))
    z16 = jnp.zeros((ROPE_HALF,), F32)
    one16 = jnp.ones((ROPE_HALF,), F32)
    lane = lambda lo, hi: jnp.concatenate([jnp.zeros((MLA_NOPE,), F32), lo, hi,
                                           jnp.zeros((HEAD_PAD - MLA_NOPE - MLA_ROPE,), F32)])
    rope_c = jnp.stack([lane(inv_freq, inv_freq), lane(-one16, z16), lane(z16, one16)])

    bias = jnp.concatenate([dt_bias_f, dt_bias_b, jnp.zeros((2 * H,), F32)])
    neg_a = jnp.concatenate([-jnp.exp(a_log_f), -jnp.exp(a_log_b), jnp.zeros((2 * H,), F32)])
    g_row_c = jnp.stack([bias, neg_a], axis=1)
    g_col_c = jnp.pad(jnp.stack([bias, neg_a]), ((0, 0), (0, HEAD_PAD - 4 * H)))

    return {
        "n1": norm1_w.reshape(1, D_MODEL),
        "w_small": w_small.astype(BF16),
        "w_gdn": w_in[:, o_gdn:o_gab].astype(BF16),
        "w_gg": w_in[:, o_gg:o_ga].astype(BF16),
        "w_ga": w_in[:, o_ga:o_gb].astype(BF16),
        "w_gb": w_in[:, o_gb:].astype(BF16),
        "w_gt": w_gab.T.astype(BF16),
        "qn": q_norm_w.reshape(1, MLA_Q_LORA),
        "w_uq": w_uq_p.reshape(MLA_Q_LORA, MLA_HEADS * HEAD_PAD).astype(BF16),
        "kvn": kv_norm_w.reshape(1, MLA_KV_LORA),
        "w_ukv": w_ukv.astype(BF16),
        "v_one": v_one,
        "rope_c": rope_c,
        "g_col_c": g_col_c,
        "g_row_c": g_row_c,
        "gnw": gdn_norm_w.reshape(1, GDN_DV),
        "w_pa": w_proj_a.astype(BF16),
        "w_pb": w_proj_b.astype(BF16),
        "w_o": w_out.astype(BF16),
        "n2": norm2_w.reshape(1, D_MODEL),
        "w_1": w_ff1.astype(BF16),
        "w_2": w_ff2.astype(BF16),
        "nf": final_norm_w.reshape(1, D_MODEL),
    }


def _tile_sizes(S):
    return dict(
        tm=min(512, S),
        tq=min(512, S),
        prep_seg=min(512, S),
        scan_seg=min(1024, S),
    )


def _layer(x, positions, conv_w, w):
    B, S, _ = x.shape
    T = B * S
    ts = _tile_sizes(S)
    tm = ts["tm"]
    x2 = x.reshape(T, D_MODEL)
    q, k, v, zg, gg, sa, sb, gcol, grow = _in_proj(x2, positions.reshape(T, 1), w, tm)

    o_a, qc, kc, vc = _attention(q.reshape(B, S, -1), k.reshape(B, S, -1), v.reshape(B, S, -1),
                                 zg.reshape(B, S, GDN_QKV), conv_w, ts["tq"])

    grow_c = grow.reshape(4 * GDN_HEADS, B, S // CHUNK, CHUNK).transpose(1, 2, 0, 3)
    o_f, o_b = _gdn(qc, kc, vc, gcol.reshape(B, S, 4 * GDN_HEADS), grow_c, ts["prep_seg"], ts["scan_seg"])

    y = _out_proj(x2, o_a.reshape(T, MLA_WIDTH), o_f.reshape(T, GDN_WIDTH), o_b.reshape(T, GDN_WIDTH),
                  gg, sa, sb, w, tm)
    return y.reshape(B, S, D_MODEL)


def kernel(x, positions, norm1_w, w_in, q_norm_w, w_uq, kv_norm_w, w_ukv, conv_w, a_log_f, dt_bias_f,
           a_log_b, dt_bias_b, gdn_norm_w, w_proj_a, w_proj_b, w_out, norm2_w, w_ff1, w_ff2,
           final_norm_w):
    assert norm1_w.shape[0] == 1, "single-layer block"
    w = _prepare_weights(norm1_w[0], w_in[0], q_norm_w[0], w_uq[0], kv_norm_w[0], w_ukv[0], a_log_f[0],
                         dt_bias_f[0], a_log_b[0], dt_bias_b[0], gdn_norm_w[0], w_proj_a[0], w_proj_b[0],
                         w_out[0], norm2_w[0], w_ff1[0], w_ff2[0], final_norm_w)
    return _layer(x, positions, conv_w[0], w)
```

```python
import functools
import math

import jax
import jax.numpy as jnp
from jax import lax
from jax.experimental import pallas as pl
from jax.experimental.pallas import tpu as pltpu

F32 = jnp.float32
BF16 = jnp.bfloat16

D_MODEL = 1024
MLA_HEADS = 8
MLA_Q_LORA = 256
MLA_KV_LORA = 128
MLA_NOPE = 64
MLA_ROPE = 32
MLA_V = 64
ROPE_THETA = 10000.0
GDN_HEADS = 4
GDN_DK = 128
GDN_DV = 128
GDN_CONV = 5
CHUNK = 64
SUB = 16
D_FF = 4 * D_MODEL
EPS = 1e-6

MLA_WIDTH = MLA_HEADS * MLA_V
GDN_QK = GDN_HEADS * GDN_DK
GDN_WIDTH = GDN_HEADS * GDN_DV
GDN_QKV = 2 * GDN_QK + GDN_WIDTH
HEAD_PAD = 128
ROPE_HALF = MLA_ROPE // 2

VMEM_LIMIT_BYTES = 56 * 1024 * 1024


def _dot(a, b):
    return jnp.dot(a, b, preferred_element_type=F32)


def _dot_nt(a, b):
    return lax.dot_general(a, b, (((1,), (1,)), ((), ())), preferred_element_type=F32)


def _dot_tn(a, b):
    return lax.dot_general(a, b, (((0,), (0,)), ((), ())), preferred_element_type=F32)


def _sigmoid(x):
    return 1.0 / (1.0 + jnp.exp(-x))


def _silu(x):
    return x * _sigmoid(x)


def _softplus(x):
    return jnp.maximum(x, 0.0) + jnp.log(1.0 + jnp.exp(-jnp.abs(x)))


def _rms(x, w):
    return x * lax.rsqrt(jnp.mean(x * x, axis=-1, keepdims=True) + EPS) * w


def _chunk_cumsum(x, axis, reverse):
    n = x.shape[axis]
    pos = lax.broadcasted_iota(jnp.int32, x.shape, axis) % CHUNK
    s = 1
    while s < CHUNK:
        if reverse:
            shifted = pltpu.roll(x, n - s, axis)
            x = x + jnp.where(pos < CHUNK - s, shifted, 0.0)
        else:
            shifted = pltpu.roll(x, s, axis)
            x = x + jnp.where(pos >= s, shifted, 0.0)
        s *= 2
    return x


def _decay_beta(z, bias, neg_a, axis):
    idx = lax.broadcasted_iota(jnp.int32, z.shape, 1 - axis)
    g = neg_a * _softplus(z + bias)
    g_f = _chunk_cumsum(jnp.where(idx < GDN_HEADS, g, 0.0), axis, reverse=False)
    g_b = _chunk_cumsum(jnp.where((idx >= GDN_HEADS) & (idx < 2 * GDN_HEADS), g, 0.0), axis, reverse=True)
    return jnp.where(idx < GDN_HEADS, g_f, jnp.where(idx < 2 * GDN_HEADS, g_b, _sigmoid(z)))


def _rope(t, cos, sin_lo, sin_hi):
    return (t * cos + pltpu.roll(t, HEAD_PAD - ROPE_HALF, 1) * sin_lo
            + pltpu.roll(t, ROPE_HALF, 1) * sin_hi)


def _in_proj_kernel(x_ref, pos_ref, n1_ref, wsm_ref, wgdn_ref, wgg_ref, wga_ref, wgb_ref, wgt_ref,
                    qn_ref, wuq_ref, kvn_ref, wukv_ref, vone_ref, ropec_ref, gcc_ref, gcr_ref,
                    q_out, k_out, v_out, zg_out, gg_out, sa_out, sb_out, gcol_out, grow_out):
    hb = _rms(x_ref[...], n1_ref[...]).astype(BF16)

    zs = _dot(hb, wsm_ref[...])
    c_q = zs[:, :MLA_Q_LORA]
    c_kv = zs[:, MLA_Q_LORA:MLA_Q_LORA + MLA_KV_LORA]
    k_r = zs[:, 384:512]
    z_gab = zs[:, 512:640]

    wide = [(sa_out, wga_ref, c0, _sigmoid) for c0 in range(0, D_MODEL, 512)]
    wide += [(sb_out, wgb_ref, c0, _sigmoid) for c0 in range(0, D_MODEL, 512)]
    wide += [(gg_out, wgg_ref, 0, _silu)]
    wide += [(zg_out, wgdn_ref, c0, None) for c0 in range(0, GDN_QKV, 512)]
    wide = iter(wide)

    def wide_piece():
        out_ref, w_ref, c0, act = next(wide)
        y = _dot(hb, w_ref[:, c0:c0 + 512])
        out_ref[:, c0:c0 + 512] = (y if act is None else act(y)).astype(BF16)

    wide_piece()
    ang = pos_ref[...].astype(F32) * ropec_ref[0:1, :]
    cos = jnp.cos(ang)
    wide_piece()
    sin = jnp.sin(ang)
    sin_lo = sin * ropec_ref[1:2, :]
    sin_hi = sin * ropec_ref[2:3, :]
    wide_piece()

    qf = _dot(_rms(c_q, qn_ref[...]).astype(BF16), wuq_ref[...])
    scale = math.log2(math.e) / math.sqrt(MLA_NOPE + MLA_ROPE)
    for h in range(MLA_HEADS):
        sl = slice(h * HEAD_PAD, (h + 1) * HEAD_PAD)
        q_out[:, sl] = (_rope(qf[:, sl], cos, sin_lo, sin_hi) * scale).astype(BF16)
        if h % 4 == 3:
            wide_piece()

    ckv = _rms(c_kv, kvn_ref[...]).astype(BF16)
    kv = _dot(ckv, wukv_ref[...])
    kr = _rope(k_r, cos, sin_lo, sin_hi)
    low = lax.broadcasted_iota(jnp.int32, (1, HEAD_PAD), 1) < MLA_NOPE
    for h in range(MLA_HEADS):
        sl = slice(h * HEAD_PAD, (h + 1) * HEAD_PAD)
        k_out[:, sl] = jnp.where(low, kv[:, sl], kr).astype(BF16)
        v_out[:, sl] = jnp.where(low, pltpu.roll(kv[:, sl], MLA_V, 1), vone_ref[...]).astype(BF16)
    wide_piece()

    col = _decay_beta(z_gab, gcc_ref[0:1, :], gcc_ref[1:2, :], axis=0)
    gcol_out[...] = col
    wide_piece()
    z_t = _dot_nt(wgt_ref[...], hb)
    grow_out[...] = _decay_beta(z_t, gcr_ref[:, 0:1], gcr_ref[:, 1:2], axis=1)
    wide_piece()
    assert next(wide, None) is None


def _in_proj(x2, pos2, w, tm):
    T = x2.shape[0]
    grid = (T // tm,)

    def full(a):
        return pl.BlockSpec(a.shape, lambda i: (0,) * a.ndim, pipeline_mode=pl.Buffered(1))

    def rows(width):
        return pl.BlockSpec((tm, width), lambda i: (i, 0))

    ins = [x2, pos2, w["n1"], w["w_small"], w["w_gdn"], w["w_gg"], w["w_ga"], w["w_gb"], w["w_gt"],
           w["qn"], w["w_uq"], w["kvn"], w["w_ukv"], w["v_one"], w["rope_c"], w["g_col_c"], w["g_row_c"]]
    in_specs = [rows(D_MODEL), rows(HEAD_PAD)] + [full(a) for a in ins[2:]]
    out_shape = [
        jax.ShapeDtypeStruct((T, MLA_HEADS * HEAD_PAD), BF16),
        jax.ShapeDtypeStruct((T, MLA_HEADS * HEAD_PAD), BF16),
        jax.ShapeDtypeStruct((T, MLA_HEADS * HEAD_PAD), BF16),
        jax.ShapeDtypeStruct((T, GDN_QKV), BF16),
        jax.ShapeDtypeStruct((T, GDN_WIDTH), BF16),
        jax.ShapeDtypeStruct((T, D_MODEL), BF16),
        jax.ShapeDtypeStruct((T, D_MODEL), BF16),
        jax.ShapeDtypeStruct((T, HEAD_PAD), F32),
        jax.ShapeDtypeStruct((4 * GDN_HEADS, T), F32),
    ]
    out_specs = [rows(MLA_HEADS * HEAD_PAD), rows(MLA_HEADS * HEAD_PAD), rows(MLA_HEADS * HEAD_PAD), rows(GDN_QKV),
                 rows(GDN_WIDTH), rows(D_MODEL), rows(D_MODEL), rows(HEAD_PAD),
                 pl.BlockSpec((4 * GDN_HEADS, tm), lambda i: (0, i))]
    return pl.pallas_call(
        _in_proj_kernel, grid=grid, in_specs=in_specs, out_specs=out_specs, out_shape=out_shape,
        name="in_proj",
        compiler_params=pltpu.CompilerParams(dimension_semantics=("parallel",),
                                             vmem_limit_bytes=VMEM_LIMIT_BYTES),
    )(*ins)


ATTN_GROUP = 2
ATTN_Q_TILES = 4
GDN_HALO = 16


def _attn_kernel(q_ref, k_ref, v_ref, *rest, tq):
    z_refs, zp_refs, zn_refs, cw_refs = rest[0:3], rest[3:6], rest[6:9], rest[9:12]
    o_ref, conv_out_refs, pad_s = rest[12], rest[13:16], rest[16]
    R = q_ref.shape[0]
    it = pl.program_id(2)
    halo = GDN_HALO

    for s, (z_ref, zp_ref, zn_ref) in enumerate(zip(z_refs, zp_refs, zn_refs)):
        pad_s[s, 0:halo, :] = jnp.where(it == 0, 0.0, zp_ref[...].astype(F32))
        pad_s[s, halo:halo + R, :] = z_ref[...].astype(F32)
        pad_s[s, halo + R:, :] = jnp.where(it == pl.num_programs(2) - 1, 0.0, zn_ref[...].astype(F32))

    stream_norm = ((True, GDN_DK ** -0.5), (True, 1.0), (False, 1.0))

    for i in range(R // tq):
        rows = slice(i * tq, (i + 1) * tq)
        for s, (cw_ref, out_ref, (l2, mult)) in enumerate(zip(cw_refs, conv_out_refs, stream_norm)):
            acc = jnp.zeros((tq, GDN_DK), F32)
            for t in range(GDN_CONV):
                off = halo + i * tq + t - GDN_CONV // 2
                acc = acc + pad_s[s, off:off + tq, :] * cw_ref[t:t + 1, :]
            y = _silu(acc)
            if l2:
                y = y * (lax.rsqrt(jnp.sum(y * y, axis=-1, keepdims=True) + EPS) * mult)
            out_ref[rows, :] = y.astype(BF16)

        scores = [_dot_nt(q_ref[rows, j * HEAD_PAD:(j + 1) * HEAD_PAD], k_ref[:, j * HEAD_PAD:(j + 1) * HEAD_PAD])
                  for j in range(ATTN_GROUP)]

        outs = []
        for j, s in enumerate(scores):
            p = jnp.exp2((s - jnp.max(s, axis=-1, keepdims=True)).astype(BF16))
            o = _dot(p, v_ref[:, j * HEAD_PAD:(j + 1) * HEAD_PAD])
            outs.append(o[:, :MLA_V] / o[:, MLA_V:MLA_V + 1])
        o_ref[rows, :] = jnp.concatenate(outs, axis=-1).astype(BF16)


def _attention(q, k, v, zg, conv_w, tq):
    B, S, _ = q.shape
    H = GDN_HEADS
    assert MLA_HEADS // ATTN_GROUP == H
    R = min(ATTN_Q_TILES * tq, S)
    hb = R // GDN_HALO
    last_hb = S // GDN_HALO - 1
    kv = pl.BlockSpec((None, S, ATTN_GROUP * HEAD_PAD), lambda b, g, r: (b, 0, g))
    z_main = [pl.BlockSpec((None, R, GDN_DK), lambda b, g, r, s=s: (b, r, s * H + g)) for s in range(3)]
    z_prev = [pl.BlockSpec((None, GDN_HALO, GDN_DK), lambda b, g, r, s=s: (b, jnp.maximum(r * hb - 1, 0), s * H + g))
              for s in range(3)]
    z_next = [pl.BlockSpec((None, GDN_HALO, GDN_DK),
                           lambda b, g, r, s=s: (b, jnp.minimum((r + 1) * hb, last_hb), s * H + g)) for s in range(3)]
    cw = [pl.BlockSpec((GDN_CONV, GDN_DK), lambda b, g, r, s=s: (0, s * H + g)) for s in range(3)]
    slab_out = pl.BlockSpec((None, R, GDN_DK), lambda b, g, r: (b, r, g))
    slab_shape = jax.ShapeDtypeStruct((B, S, H * GDN_DK), BF16)
    return pl.pallas_call(
        functools.partial(_attn_kernel, tq=tq), grid=(B, MLA_HEADS // ATTN_GROUP, S // R),
        in_specs=[pl.BlockSpec((None, R, ATTN_GROUP * HEAD_PAD), lambda b, g, r: (b, r, g)), kv, kv]
                 + z_main + z_prev + z_next + cw,
        out_specs=[pl.BlockSpec((None, R, ATTN_GROUP * MLA_V), lambda b, g, r: (b, r, g)),
                   slab_out, slab_out, slab_out],
        out_shape=[jax.ShapeDtypeStruct((B, S, MLA_WIDTH), BF16), slab_shape, slab_shape, slab_shape],
        scratch_shapes=[pltpu.VMEM((3, R + 2 * GDN_HALO, GDN_DK), F32)],
        name="mla_attention",
        compiler_params=pltpu.CompilerParams(dimension_semantics=("parallel", "parallel", "parallel"),
                                             vmem_limit_bytes=VMEM_LIMIT_BYTES),
    )(q, k, v, *([zg] * 9), *([conv_w] * 3))


def _bmm(a, b):
    return jnp.einsum("kij,kjl->kil", a, b, preferred_element_type=F32)


def _bmm_nt(a, b):
    return jnp.einsum("kid,kjd->kij", a, b, preferred_element_type=F32)


def _unit_tri_solve(l_mat, rhs, on_diag, eye):
    n1 = jnp.where(on_diag, -l_mat, 0.0)
    l_off = jnp.where(on_diag, 0.0, l_mat)
    n2 = _bmm(n1, n1)
    d = eye + n1
    d = d + _bmm(d, n2)
    n4 = _bmm(n2, n2)
    d = d + _bmm(d, n4)
    n8 = _bmm(n4, n4)
    d = d + _bmm(d, n8)
    db = d.astype(BF16)
    m = _bmm(db, l_off.astype(BF16))
    y0 = _bmm(db, rhs.astype(BF16))
    mb = m.astype(BF16)
    m2 = _bmm(mb, mb)
    z1 = y0 - _bmm(mb, y0.astype(BF16))
    return z1 + _bmm(m2.astype(BF16), z1.astype(BF16))


def _gdn_prep_kernel(qc_ref, kc_ref, vc_ref, gcol_ref, grow_ref,
                     uf_ref, wf_ref, qef_ref, kdf_ref, qkf_ref,
                     ub_ref, wb_ref, qeb_ref, kdb_ref, qkb_ref, et_ref):
    seg = qc_ref.shape[0]
    K = seg // CHUNK

    row = lax.broadcasted_iota(jnp.int32, (CHUNK, CHUNK), 0)
    col = lax.broadcasted_iota(jnp.int32, (CHUNK, CHUNK), 1)
    on_diag = (row // SUB) == (col // SUB)
    eye = (row == col).astype(F32)
    gc = gcol_ref[...].reshape(K, CHUNK, HEAD_PAD)

    systems, rhs_all, stores = [], [], []
    for h in range(GDN_HEADS):
        hs = slice(h * GDN_DK, (h + 1) * GDN_DK)
        qb = qc_ref[:, hs].reshape(K, CHUNK, GDN_DK)
        kb = kc_ref[:, hs].reshape(K, CHUNK, GDN_DK)
        q3, k3, v3 = qb.astype(F32), kb.astype(F32), vc_ref[:, hs].astype(F32).reshape(K, CHUNK, GDN_DV)
        gram = _bmm_nt(kb, kb)
        qk = _bmm_nt(qb, kb)
        for d, keep_incl, keep_strict, total_row, refs in (
                (0, row >= col, row > col, CHUNK - 1, (uf_ref, wf_ref, qef_ref, kdf_ref, qkf_ref)),
                (1, row <= col, row < col, 0, (ub_ref, wb_ref, qeb_ref, kdb_ref, qkb_ref))):
            u_ref, w_ref, qe_ref, kd_ref, qk_ref = refs
            lane = d * GDN_HEADS + h
            g_col = gc[:, :, lane:lane + 1]
            g_row = grow_ref[:, lane:lane + 1, :]
            beta = gc[:, :, 2 * GDN_HEADS + lane:2 * GDN_HEADS + lane + 1]
            decay = jnp.where(keep_incl, jnp.exp(jnp.where(keep_incl, g_col - g_row, 0.0)), 0.0)
            e_g = jnp.exp(g_col)
            g_total = g_col[:, total_row:total_row + 1, :]
            systems.append(jnp.where(keep_strict, gram * beta * decay, 0.0))
            rhs_all.append(jnp.concatenate([v3 * beta, k3 * (beta * e_g)], axis=-1))
            stores.append((u_ref, w_ref, hs))
            qe_ref[:, hs] = (q3 * e_g).astype(BF16).reshape(seg, GDN_DK)
            kd_ref[:, hs] = (k3 * jnp.exp(g_total - g_col)).astype(BF16).reshape(seg, GDN_DK)
            qk_ref[h] = (qk * decay).astype(BF16).reshape(seg, CHUNK)
            et_ref[:, lane:lane + 1, :] = jnp.broadcast_to(jnp.exp(g_total), (K, 1, GDN_DV))

    sol = _unit_tri_solve(jnp.concatenate(systems, axis=0), jnp.concatenate(rhs_all, axis=0), on_diag, eye)
    for i, (u_ref, w_ref, hs) in enumerate(stores):
        part = sol[i * K:(i + 1) * K]
        u_ref[:, hs] = part[:, :, :GDN_DV].astype(BF16).reshape(seg, GDN_DV)
        w_ref[:, hs] = part[:, :, GDN_DV:].astype(BF16).reshape(seg, GDN_DK)


def _gdn_scan_kernel(uf_ref, wf_ref, qef_ref, kdf_ref, qkf_ref, etf_ref,
                     ub_ref, wb_ref, qeb_ref, kdb_ref, qkb_ref, etb_ref,
                     of_ref, ob_ref, st_ref):
    seg = uf_ref.shape[0]
    K = seg // CHUNK

    @pl.when(pl.program_id(1) == 0)
    def _():
        st_ref[...] = jnp.zeros(st_ref.shape, F32)

    dirs = ((uf_ref, wf_ref, qef_ref, kdf_ref, qkf_ref, etf_ref, of_ref),
            (ub_ref, wb_ref, qeb_ref, kdb_ref, qkb_ref, etb_ref, ob_ref))

    def step(t, carry):
        chains = []
        for d, (u_ref, w_ref, qe_ref, kd_ref, qk_ref, et_ref, o_ref) in enumerate(dirs):
            c = t if d == 0 else K - 1 - t
            rows = pl.ds(pl.multiple_of(c * CHUNK, CHUNK), CHUNK)
            for h in range(GDN_HEADS):
                chains.append((d * GDN_HEADS + h, h, c, rows, slice(h * GDN_DK, (h + 1) * GDN_DK),
                               u_ref, w_ref, qe_ref, kd_ref, qk_ref, et_ref, o_ref))
        sb = [st_ref[ch[0]].astype(BF16) for ch in chains]
        v_new = [u_ref[rows, hs] - _dot(w_ref[rows, hs], s)
                 for s, (_, _, _, rows, hs, u_ref, w_ref, *_) in zip(sb, chains)]
        o_inter = [_dot(qe_ref[rows, hs], s)
                   for s, (_, _, _, rows, hs, _, _, qe_ref, *_) in zip(sb, chains)]
        vb = [v.astype(BF16) for v in v_new]
        for v, oi, (lane, h, c, rows, hs, _, _, _, kd_ref, qk_ref, et_ref, o_ref) in zip(vb, o_inter, chains):
            o_ref[rows, hs] = (oi + _dot(qk_ref[h, rows, :], v)).astype(BF16)
            st_ref[lane] = st_ref[lane] * et_ref[c, lane:lane + 1, :] + _dot_tn(kd_ref[rows, hs], v)
        return carry

    lax.fori_loop(0, K, step, 0, unroll=True)


def _gdn(qc, kc, vc, gcol, grow, prep_seg, scan_seg):
    B, S, _ = qc.shape
    H = GDN_HEADS

    wide = lambda dt: jax.ShapeDtypeStruct((B, S, H * GDN_DK), dt)
    qk_shape = jax.ShapeDtypeStruct((B, H, S, CHUNK), BF16)
    et_shape = jax.ShapeDtypeStruct((B, S // CHUNK, 2 * H, GDN_DV), F32)
    dir_shapes = [wide(BF16), wide(BF16), wide(BF16), wide(BF16), qk_shape]
    here3 = lambda b, j: (b, j, 0)
    here4 = lambda b, j: (b, 0, j, 0)

    seg, K = prep_seg, prep_seg // CHUNK
    seg_spec = lambda imap: pl.BlockSpec((None, seg, H * GDN_DK), imap)
    qk_spec = lambda imap: pl.BlockSpec((None, H, seg, CHUNK), imap)
    et_spec = lambda imap: pl.BlockSpec((None, K, 2 * H, GDN_DV), imap)
    dir_specs = [seg_spec(here3)] * 4 + [qk_spec(here4)]
    prep = pl.pallas_call(
        _gdn_prep_kernel, grid=(B, S // seg),
        in_specs=[seg_spec(here3), seg_spec(here3), seg_spec(here3),
                  pl.BlockSpec((None, seg, HEAD_PAD), here3),
                  pl.BlockSpec((None, K, 4 * H, CHUNK), lambda b, j: (b, j, 0, 0))],
        out_specs=dir_specs + dir_specs + [et_spec(lambda b, j: (b, j, 0, 0))],
        out_shape=dir_shapes + dir_shapes + [et_shape],
        name="gdn_prep",
        compiler_params=pltpu.CompilerParams(dimension_semantics=("parallel", "parallel"),
                                             vmem_limit_bytes=VMEM_LIMIT_BYTES),
    )(qc, kc, vc, gcol, grow)
    fwd, bwd, et = prep[:5], prep[5:10], prep[10]

    seg, K = scan_seg, scan_seg // CHUNK
    nseg = S // seg
    seg_spec = lambda imap: pl.BlockSpec((None, seg, H * GDN_DK), imap)
    qk_spec = lambda imap: pl.BlockSpec((None, H, seg, CHUNK), imap)
    et_spec = lambda imap: pl.BlockSpec((None, K, 2 * H, GDN_DV), imap)
    back3 = lambda b, j: (b, nseg - 1 - j, 0)
    back4 = lambda b, j: (b, 0, nseg - 1 - j, 0)
    return pl.pallas_call(
        _gdn_scan_kernel, grid=(B, nseg),
        in_specs=[seg_spec(here3)] * 4 + [qk_spec(here4), et_spec(lambda b, j: (b, j, 0, 0))]
                 + [seg_spec(back3)] * 4 + [qk_spec(back4), et_spec(lambda b, j: (b, nseg - 1 - j, 0, 0))],
        out_specs=[seg_spec(here3), seg_spec(back3)],
        out_shape=[wide(BF16), wide(BF16)],
        scratch_shapes=[pltpu.VMEM((2 * H, GDN_DK, GDN_DV), F32)],
        name="gdn_scan",
        compiler_params=pltpu.CompilerParams(dimension_semantics=("parallel", "arbitrary"),
                                             vmem_limit_bytes=VMEM_LIMIT_BYTES),
    )(*fwd, et, *bwd, et)


def _out_kernel(x_ref, oa_ref, of_ref, ob_ref, gg_ref, sa_ref, sb_ref, gnw_ref, wpa_ref, wpb_ref, wo_ref,
                n2_ref, w1_ref, w2_ref, nf_ref, y_ref):
    parts = []
    for h in range(GDN_HEADS):
        hs = slice(h * GDN_DV, (h + 1) * GDN_DV)
        o = of_ref[:, hs].astype(F32) + ob_ref[:, hs].astype(F32)
        parts.append((_rms(o, gnw_ref[...]) * gg_ref[:, hs].astype(F32)).astype(BF16))
    o_b = jnp.concatenate(parts, axis=-1)

    merged = (sa_ref[...].astype(F32) * _dot(oa_ref[...], wpa_ref[...])
              + sb_ref[...].astype(F32) * _dot(o_b, wpb_ref[...]))
    x1 = x_ref[...] + _dot(merged.astype(BF16), wo_ref[...])
    h2 = _rms(x1, n2_ref[...]).astype(BF16)
    a = jnp.maximum(_dot(h2, w1_ref[...]), 0.0)
    x2 = x1 + _dot((a * a).astype(BF16), w2_ref[...])
    y_ref[...] = _rms(x2, nf_ref[...])


def _out_proj(x2, oa, o_f, o_b, gg, sa, sb, w, tm):
    T = x2.shape[0]

    def full(a):
        return pl.BlockSpec(a.shape, lambda i: (0,) * a.ndim, pipeline_mode=pl.Buffered(1))

    def rows(width):
        return pl.BlockSpec((tm, width), lambda i: (i, 0))

    consts = [w["gnw"], w["w_pa"], w["w_pb"], w["w_o"], w["n2"], w["w_1"], w["w_2"], w["nf"]]
    return pl.pallas_call(
        _out_kernel, grid=(T // tm,),
        in_specs=[rows(D_MODEL), rows(MLA_WIDTH), rows(GDN_WIDTH), rows(GDN_WIDTH), rows(GDN_WIDTH),
                  rows(D_MODEL), rows(D_MODEL)] + [full(a) for a in consts],
        out_specs=rows(D_MODEL),
        out_shape=jax.ShapeDtypeStruct((T, D_MODEL), F32),
        name="merge_mlp",
        compiler_params=pltpu.CompilerParams(dimension_semantics=("parallel",),
                                             vmem_limit_bytes=VMEM_LIMIT_BYTES),
    )(x2, oa, o_f, o_b, gg, sa, sb, *consts)


def _prepare_weights(norm1_w, w_in, q_norm_w, w_uq, kv_norm_w, w_ukv, a_log_f, dt_bias_f, a_log_b,
                     dt_bias_b, gdn_norm_w, w_proj_a, w_proj_b, w_out, norm2_w, w_ff1, w_ff2, final_norm_w):
    H = GDN_HEADS
    o_kv = MLA_Q_LORA
    o_kr = o_kv + MLA_KV_LORA
    o_gdn = o_kr + MLA_ROPE
    o_gab = o_gdn + GDN_QKV
    o_gg = o_gab + 4 * H
    o_ga = o_gg + GDN_WIDTH
    o_gb = o_ga + D_MODEL
    zeros = lambda n: jnp.zeros((D_MODEL, n), F32)
    w_kr = jnp.concatenate([zeros(MLA_NOPE), w_in[:, o_kr:o_gdn], zeros(HEAD_PAD - MLA_NOPE - MLA_ROPE)], 1)
    w_gab = w_in[:, o_gab:o_gg]
    w_small = jnp.concatenate([w_in[:, :o_kr], w_kr, w_gab, zeros(HEAD_PAD - 4 * H)], 1)

    dq = MLA_NOPE + MLA_ROPE
    w_uq_p = jnp.pad(w_uq.reshape(MLA_Q_LORA, MLA_HEADS, dq), ((0, 0), (0, 0), (0, HEAD_PAD - dq)))
    assert MLA_NOPE + MLA_V == HEAD_PAD and MLA_NOPE == MLA_V
    v_one = (jnp.arange(HEAD_PAD) == MLA_V).astype(F32).reshape(1, HEAD_PAD)

    inv_freq = 1.0 / (ROPE_THETA ** (jnp.arange(0, MLA_ROPE, 2, dtype=F32) / MLA_ROPE))
    z16 = jnp.zeros((ROPE_HALF,), F32)
    one16 = jnp.ones((ROPE_HALF,), F32)
    lane = lambda lo, hi: jnp.concatenate([jnp.zeros((MLA_NOPE,), F32), lo, hi,
                                           jnp.zeros((HEAD_PAD - MLA_NOPE - MLA_ROPE,), F32)])
    rope_c = jnp.stack([lane(inv_freq, inv_freq), lane(-one16, z16), lane(z16, one16)])

    bias = jnp.concatenate([dt_bias_f, dt_bias_b, jnp.zeros((2 * H,), F32)])
    neg_a = jnp.concatenate([-jnp.exp(a_log_f), -jnp.exp(a_log_b), jnp.zeros((2 * H,), F32)])
    g_row_c = jnp.stack([bias, neg_a], axis=1)
    g_col_c = jnp.pad(jnp.stack([bias, neg_a]), ((0, 0), (0, HEAD_PAD - 4 * H)))

    return {
        "n1": norm1_w.reshape(1, D_MODEL),
        "w_small": w_small.astype(BF16),
        "w_gdn": w_in[:, o_gdn:o_gab].astype(BF16),
        "w_gg": w_in[:, o_gg:o_ga].astype(BF16),
        "w_ga": w_in[:, o_ga:o_gb].astype(BF16),
        "w_gb": w_in[:, o_gb:].astype(BF16),
        "w_gt": w_gab.T.astype(BF16),
        "qn": q_norm_w.reshape(1, MLA_Q_LORA),
        "w_uq": w_uq_p.reshape(MLA_Q_LORA, MLA_HEADS * HEAD_PAD).astype(BF16),
        "kvn": kv_norm_w.reshape(1, MLA_KV_LORA),
        "w_ukv": w_ukv.astype(BF16),
        "v_one": v_one,
        "rope_c": rope_c,
        "g_col_c": g_col_c,
        "g_row_c": g_row_c,
        "gnw": gdn_norm_w.reshape(1, GDN_DV),
        "w_pa": w_proj_a.astype(BF16),
        "w_pb": w_proj_b.astype(BF16),
        "w_o": w_out.astype(BF16),
        "n2": norm2_w.reshape(1, D_MODEL),
        "w_1": w_ff1.astype(BF16),
        "w_2": w_ff2.astype(BF16),
        "nf": final_norm_w.reshape(1, D_MODEL),
    }


def _tile_sizes(S):
    return dict(
        tm=min(512, S),
        tq=min(512, S),
        prep_seg=min(512, S),
        scan_seg=min(1024, S),
    )


def _layer(x, positions, conv_w, w):
    B, S, _ = x.shape
    T = B * S
    ts = _tile_sizes(S)
    tm = ts["tm"]
    x2 = x.reshape(T, D_MODEL)
    pos = jnp.broadcast_to(positions.reshape(T, 1), (T, HEAD_PAD))
    q, k, v, zg, gg, sa, sb, gcol, grow = _in_proj(x2, pos, w, tm)

    o_a, qc, kc, vc = _attention(q.reshape(B, S, -1), k.reshape(B, S, -1), v.reshape(B, S, -1),
                                 zg.reshape(B, S, GDN_QKV), conv_w, ts["tq"])

    grow_c = grow.reshape(4 * GDN_HEADS, B, S // CHUNK, CHUNK).transpose(1, 2, 0, 3)
    o_f, o_b = _gdn(qc, kc, vc, gcol.reshape(B, S, HEAD_PAD), grow_c, ts["prep_seg"], ts["scan_seg"])

    y = _out_proj(x2, o_a.reshape(T, MLA_WIDTH), o_f.reshape(T, GDN_WIDTH), o_b.reshape(T, GDN_WIDTH),
                  gg, sa, sb, w, tm)
    return y.reshape(B, S, D_MODEL)


def kernel(x, positions, norm1_w, w_in, q_norm_w, w_uq, kv_norm_w, w_ukv, conv_w, a_log_f, dt_bias_f,
           a_log_b, dt_bias_b, gdn_norm_w, w_proj_a, w_proj_b, w_out, norm2_w, w_ff1, w_ff2,
           final_norm_w):
    assert norm1_w.shape[0] == 1, "single-layer block"
    w = _prepare_weights(norm1_w[0], w_in[0], q_norm_w[0], w_uq[0], kv_norm_w[0], w_ukv[0], a_log_f[0],
                         dt_bias_f[0], a_log_b[0], dt_bias_b[0], gdn_norm_w[0], w_proj_a[0], w_proj_b[0],
                         w_out[0], norm2_w[0], w_ff1[0], w_ff2[0], final_norm_w)
    return _layer(x, positions, conv_w[0], w)
```

```python
import functools
import math

import jax
import jax.numpy as jnp
from jax import lax
from jax.experimental import pallas as pl
from jax.experimental.pallas import tpu as pltpu

F32 = jnp.float32
BF16 = jnp.bfloat16

D_MODEL = 1024
MLA_HEADS = 8
MLA_Q_LORA = 256
MLA_KV_LORA = 128
MLA_NOPE = 64
MLA_ROPE = 32
MLA_V = 64
ROPE_THETA = 10000.0
GDN_HEADS = 4
GDN_DK = 128
GDN_DV = 128
GDN_CONV = 5
CHUNK = 64
SUB = 16
D_FF = 4 * D_MODEL
EPS = 1e-6

MLA_WIDTH = MLA_HEADS * MLA_V
GDN_QK = GDN_HEADS * GDN_DK
GDN_WIDTH = GDN_HEADS * GDN_DV
GDN_QKV = 2 * GDN_QK + GDN_WIDTH
HEAD_PAD = 128
ROPE_HALF = MLA_ROPE // 2

VMEM_LIMIT_BYTES = 56 * 1024 * 1024


def _dot(a, b):
    return jnp.dot(a, b, preferred_element_type=F32)


def _dot_nt(a, b):
    return lax.dot_general(a, b, (((1,), (1,)), ((), ())), preferred_element_type=F32)


def _dot_tn(a, b):
    return lax.dot_general(a, b, (((0,), (0,)), ((), ())), preferred_element_type=F32)


def _sigmoid(x):
    return 1.0 / (1.0 + jnp.exp(-x))


def _silu(x):
    return x * _sigmoid(x)


def _softplus(x):
    return jnp.maximum(x, 0.0) + jnp.log(1.0 + jnp.exp(-jnp.abs(x)))


def _rms(x, w):
    return x * lax.rsqrt(jnp.mean(x * x, axis=-1, keepdims=True) + EPS) * w


def _chunk_cumsum(x, axis, reverse):
    n = x.shape[axis]
    pos = lax.broadcasted_iota(jnp.int32, x.shape, axis) % CHUNK
    s = 1
    while s < CHUNK:
        if reverse:
            shifted = pltpu.roll(x, n - s, axis)
            x = x + jnp.where(pos < CHUNK - s, shifted, 0.0)
        else:
            shifted = pltpu.roll(x, s, axis)
            x = x + jnp.where(pos >= s, shifted, 0.0)
        s *= 2
    return x


def _decay_beta(z, bias, neg_a, axis):
    idx = lax.broadcasted_iota(jnp.int32, z.shape, 1 - axis)
    g = neg_a * _softplus(z + bias)
    g_f = _chunk_cumsum(jnp.where(idx < GDN_HEADS, g, 0.0), axis, reverse=False)
    g_b = _chunk_cumsum(jnp.where((idx >= GDN_HEADS) & (idx < 2 * GDN_HEADS), g, 0.0), axis, reverse=True)
    return jnp.where(idx < GDN_HEADS, g_f, jnp.where(idx < 2 * GDN_HEADS, g_b, _sigmoid(z)))


def _rope(t, cos, sin_lo, sin_hi):
    return (t * cos + pltpu.roll(t, HEAD_PAD - ROPE_HALF, 1) * sin_lo
            + pltpu.roll(t, ROPE_HALF, 1) * sin_hi)


def _in_proj_kernel(x_ref, pos_ref, n1_ref, wsm_ref, wgdn_ref, wgg_ref, wga_ref, wgb_ref, wgt_ref,
                    qn_ref, wuq_ref, kvn_ref, wukv_ref, vone_ref, ropec_ref, gcc_ref, gcr_ref,
                    q_out, k_out, v_out, zg_out, gg_out, sa_out, sb_out, gcol_out, grow_out):
    hb = _rms(x_ref[...], n1_ref[...]).astype(BF16)

    zs = _dot(hb, wsm_ref[...])
    c_q = zs[:, :MLA_Q_LORA]
    c_kv = zs[:, MLA_Q_LORA:MLA_Q_LORA + MLA_KV_LORA]
    k_r = zs[:, 384:512]
    z_gab = zs[:, 512:640]

    wide = [(sa_out, wga_ref, c0, _sigmoid) for c0 in range(0, D_MODEL, 512)]
    wide += [(sb_out, wgb_ref, c0, _sigmoid) for c0 in range(0, D_MODEL, 512)]
    wide += [(gg_out, wgg_ref, 0, _silu)]
    wide += [(zg_out, wgdn_ref, c0, None) for c0 in range(0, GDN_QKV, 512)]
    wide = iter(wide)

    def wide_piece():
        out_ref, w_ref, c0, act = next(wide)
        y = _dot(hb, w_ref[:, c0:c0 + 512])
        out_ref[:, c0:c0 + 512] = (y if act is None else act(y)).astype(BF16)

    wide_piece()
    ang = pos_ref[...].astype(F32) * ropec_ref[0:1, :]
    cos = jnp.cos(ang)
    wide_piece()
    sin = jnp.sin(ang)
    sin_lo = sin * ropec_ref[1:2, :]
    sin_hi = sin * ropec_ref[2:3, :]
    wide_piece()

    qf = _dot(_rms(c_q, qn_ref[...]).astype(BF16), wuq_ref[...])
    scale = math.log2(math.e) / math.sqrt(MLA_NOPE + MLA_ROPE)
    for h in range(MLA_HEADS):
        sl = slice(h * HEAD_PAD, (h + 1) * HEAD_PAD)
        q_out[:, sl] = (_rope(qf[:, sl], cos, sin_lo, sin_hi) * scale).astype(BF16)
        if h % 4 == 3:
            wide_piece()

    ckv = _rms(c_kv, kvn_ref[...]).astype(BF16)
    kv = _dot(ckv, wukv_ref[...])
    kr = _rope(k_r, cos, sin_lo, sin_hi)
    low = lax.broadcasted_iota(jnp.int32, (1, HEAD_PAD), 1) < MLA_NOPE
    for h in range(MLA_HEADS):
        sl = slice(h * HEAD_PAD, (h + 1) * HEAD_PAD)
        k_out[:, sl] = jnp.where(low, kv[:, sl], kr).astype(BF16)
        v_out[:, sl] = jnp.where(low, pltpu.roll(kv[:, sl], MLA_V, 1), vone_ref[...]).astype(BF16)
    wide_piece()

    col = _decay_beta(z_gab, gcc_ref[0:1, :], gcc_ref[1:2, :], axis=0)
    gcol_out[...] = col
    wide_piece()
    z_t = _dot_nt(wgt_ref[...], hb)
    grow_out[...] = _decay_beta(z_t, gcr_ref[:, 0:1], gcr_ref[:, 1:2], axis=1)
    wide_piece()
    assert next(wide, None) is None


def _in_proj(x2, pos2, w, tm):
    T = x2.shape[0]
    grid = (T // tm,)

    def full(a):
        return pl.BlockSpec(a.shape, lambda i: (0,) * a.ndim, pipeline_mode=pl.Buffered(1))

    def rows(width):
        return pl.BlockSpec((tm, width), lambda i: (i, 0))

    ins = [x2, pos2, w["n1"], w["w_small"], w["w_gdn"], w["w_gg"], w["w_ga"], w["w_gb"], w["w_gt"],
           w["qn"], w["w_uq"], w["kvn"], w["w_ukv"], w["v_one"], w["rope_c"], w["g_col_c"], w["g_row_c"]]
    in_specs = [rows(D_MODEL), rows(1)] + [full(a) for a in ins[2:]]
    out_shape = [
        jax.ShapeDtypeStruct((T, MLA_HEADS * HEAD_PAD), BF16),
        jax.ShapeDtypeStruct((T, MLA_HEADS * HEAD_PAD), BF16),
        jax.ShapeDtypeStruct((T, MLA_HEADS * HEAD_PAD), BF16),
        jax.ShapeDtypeStruct((T, GDN_QKV), BF16),
        jax.ShapeDtypeStruct((T, GDN_WIDTH), BF16),
        jax.ShapeDtypeStruct((T, D_MODEL), BF16),
        jax.ShapeDtypeStruct((T, D_MODEL), BF16),
        jax.ShapeDtypeStruct((T, HEAD_PAD), F32),
        jax.ShapeDtypeStruct((4 * GDN_HEADS, T), F32),
    ]
    out_specs = [rows(MLA_HEADS * HEAD_PAD), rows(MLA_HEADS * HEAD_PAD), rows(MLA_HEADS * HEAD_PAD), rows(GDN_QKV),
                 rows(GDN_WIDTH), rows(D_MODEL), rows(D_MODEL), rows(HEAD_PAD),
                 pl.BlockSpec((4 * GDN_HEADS, tm), lambda i: (0, i))]
    return pl.pallas_call(
        _in_proj_kernel, grid=grid, in_specs=in_specs, out_specs=out_specs, out_shape=out_shape,
        name="in_proj",
        compiler_params=pltpu.CompilerParams(dimension_semantics=("parallel",),
                                             vmem_limit_bytes=VMEM_LIMIT_BYTES),
    )(*ins)


ATTN_GROUP = 2
ATTN_Q_TILES = 4
GDN_HALO = 16


def _attn_kernel(q_ref, k_ref, v_ref, *rest, tq):
    z_refs, zp_refs, zn_refs, cw_refs = rest[0:3], rest[3:6], rest[6:9], rest[9:12]
    o_ref, conv_out_refs, pad_s = rest[12], rest[13:16], rest[16]
    R = q_ref.shape[0]
    it = pl.program_id(2)
    halo = GDN_HALO

    for s, (z_ref, zp_ref, zn_ref) in enumerate(zip(z_refs, zp_refs, zn_refs)):
        pad_s[s, 0:halo, :] = jnp.where(it == 0, 0.0, zp_ref[...].astype(F32))
        pad_s[s, halo:halo + R, :] = z_ref[...].astype(F32)
        pad_s[s, halo + R:, :] = jnp.where(it == pl.num_programs(2) - 1, 0.0, zn_ref[...].astype(F32))

    stream_norm = ((True, GDN_DK ** -0.5), (True, 1.0), (False, 1.0))

    for i in range(R // tq):
        rows = slice(i * tq, (i + 1) * tq)
        for s, (cw_ref, out_ref, (l2, mult)) in enumerate(zip(cw_refs, conv_out_refs, stream_norm)):
            acc = jnp.zeros((tq, GDN_DK), F32)
            for t in range(GDN_CONV):
                off = halo + i * tq + t - GDN_CONV // 2
                acc = acc + pad_s[s, off:off + tq, :] * cw_ref[t:t + 1, :]
            y = _silu(acc)
            if l2:
                y = y * (lax.rsqrt(jnp.sum(y * y, axis=-1, keepdims=True) + EPS) * mult)
            out_ref[rows, :] = y.astype(BF16)

        scores = [_dot_nt(q_ref[rows, j * HEAD_PAD:(j + 1) * HEAD_PAD], k_ref[:, j * HEAD_PAD:(j + 1) * HEAD_PAD])
                  for j in range(ATTN_GROUP)]

        outs = []
        for j, s in enumerate(scores):
            p = jnp.exp2((s - jnp.max(s, axis=-1, keepdims=True)).astype(BF16))
            o = _dot(p, v_ref[:, j * HEAD_PAD:(j + 1) * HEAD_PAD])
            outs.append(o[:, :MLA_V] / o[:, MLA_V:MLA_V + 1])
        o_ref[rows, :] = jnp.concatenate(outs, axis=-1).astype(BF16)


def _attention(q, k, v, zg, conv_w, tq):
    B, S, _ = q.shape
    H = GDN_HEADS
    assert MLA_HEADS // ATTN_GROUP == H
    R = min(ATTN_Q_TILES * tq, S)
    hb = R // GDN_HALO
    last_hb = S // GDN_HALO - 1
    kv = pl.BlockSpec((None, S, ATTN_GROUP * HEAD_PAD), lambda b, g, r: (b, 0, g))
    z_main = [pl.BlockSpec((None, R, GDN_DK), lambda b, g, r, s=s: (b, r, s * H + g)) for s in range(3)]
    z_prev = [pl.BlockSpec((None, GDN_HALO, GDN_DK), lambda b, g, r, s=s: (b, jnp.maximum(r * hb - 1, 0), s * H + g))
              for s in range(3)]
    z_next = [pl.BlockSpec((None, GDN_HALO, GDN_DK),
                           lambda b, g, r, s=s: (b, jnp.minimum((r + 1) * hb, last_hb), s * H + g)) for s in range(3)]
    cw = [pl.BlockSpec((GDN_CONV, GDN_DK), lambda b, g, r, s=s: (0, s * H + g)) for s in range(3)]
    slab_out = pl.BlockSpec((None, R, GDN_DK), lambda b, g, r: (b, r, g))
    slab_shape = jax.ShapeDtypeStruct((B, S, H * GDN_DK), BF16)
    return pl.pallas_call(
        functools.partial(_attn_kernel, tq=tq), grid=(B, MLA_HEADS // ATTN_GROUP, S // R),
        in_specs=[pl.BlockSpec((None, R, ATTN_GROUP * HEAD_PAD), lambda b, g, r: (b, r, g)), kv, kv]
                 + z_main + z_prev + z_next + cw,
        out_specs=[pl.BlockSpec((None, R, ATTN_GROUP * MLA_V), lambda b, g, r: (b, r, g)),
                   slab_out, slab_out, slab_out],
        out_shape=[jax.ShapeDtypeStruct((B, S, MLA_WIDTH), BF16), slab_shape, slab_shape, slab_shape],
        scratch_shapes=[pltpu.VMEM((3, R + 2 * GDN_HALO, GDN_DK), F32)],
        name="mla_attention",
        compiler_params=pltpu.CompilerParams(dimension_semantics=("parallel", "parallel", "parallel"),
                                             vmem_limit_bytes=VMEM_LIMIT_BYTES),
    )(q, k, v, *([zg] * 9), *([conv_w] * 3))


def _bmm(a, b):
    return jnp.einsum("kij,kjl->kil", a, b, preferred_element_type=F32)


def _bmm_nt(a, b):
    return jnp.einsum("kid,kjd->kij", a, b, preferred_element_type=F32)


def _unit_tri_solve(l_mat, rhs, on_diag, eye):
    n1 = jnp.where(on_diag, -l_mat, 0.0)
    l_off = jnp.where(on_diag, 0.0, l_mat)
    n2 = _bmm(n1, n1)
    d = eye + n1
    d = d + _bmm(d, n2)
    n4 = _bmm(n2, n2)
    d = d + _bmm(d, n4)
    n8 = _bmm(n4, n4)
    d = d + _bmm(d, n8)
    db = d.astype(BF16)
    m = _bmm(db, l_off.astype(BF16))
    y0 = _bmm(db, rhs.astype(BF16))
    mb = m.astype(BF16)
    m2 = _bmm(mb, mb)
    z1 = y0 - _bmm(mb, y0.astype(BF16))
    return z1 + _bmm(m2.astype(BF16), z1.astype(BF16))


def _gdn_prep_kernel(qc_ref, kc_ref, vc_ref, gcol_ref, grow_ref,
                     uf_ref, wf_ref, qef_ref, kdf_ref, qkf_ref,
                     ub_ref, wb_ref, qeb_ref, kdb_ref, qkb_ref, et_ref):
    seg = qc_ref.shape[0]
    K = seg // CHUNK

    row = lax.broadcasted_iota(jnp.int32, (CHUNK, CHUNK), 0)
    col = lax.broadcasted_iota(jnp.int32, (CHUNK, CHUNK), 1)
    on_diag = (row // SUB) == (col // SUB)
    eye = (row == col).astype(F32)
    gc = gcol_ref[...].reshape(K, CHUNK, HEAD_PAD)

    systems, rhs_all, stores = [], [], []
    for h in range(GDN_HEADS):
        hs = slice(h * GDN_DK, (h + 1) * GDN_DK)
        qb = qc_ref[:, hs].reshape(K, CHUNK, GDN_DK)
        kb = kc_ref[:, hs].reshape(K, CHUNK, GDN_DK)
        q3, k3, v3 = qb.astype(F32), kb.astype(F32), vc_ref[:, hs].astype(F32).reshape(K, CHUNK, GDN_DV)
        gram = _bmm_nt(kb, kb)
        qk = _bmm_nt(qb, kb)
        for d, keep_incl, keep_strict, total_row, refs in (
                (0, row >= col, row > col, CHUNK - 1, (uf_ref, wf_ref, qef_ref, kdf_ref, qkf_ref)),
                (1, row <= col, row < col, 0, (ub_ref, wb_ref, qeb_ref, kdb_ref, qkb_ref))):
            u_ref, w_ref, qe_ref, kd_ref, qk_ref = refs
            lane = d * GDN_HEADS + h
            g_col = gc[:, :, lane:lane + 1]
            g_row = grow_ref[:, lane:lane + 1, :]
            beta = gc[:, :, 2 * GDN_HEADS + lane:2 * GDN_HEADS + lane + 1]
            decay = jnp.where(keep_incl, jnp.exp(jnp.where(keep_incl, g_col - g_row, 0.0)), 0.0)
            e_g = jnp.exp(g_col)
            g_total = g_col[:, total_row:total_row + 1, :]
            systems.append(jnp.where(keep_strict, gram * beta * decay, 0.0))
            rhs_all.append(jnp.concatenate([v3 * beta, k3 * (beta * e_g)], axis=-1))
            stores.append((u_ref, w_ref, hs))
            qe_ref[:, hs] = (q3 * e_g).astype(BF16).reshape(seg, GDN_DK)
            kd_ref[:, hs] = (k3 * jnp.exp(g_total - g_col)).astype(BF16).reshape(seg, GDN_DK)
            qk_ref[h] = (qk * decay).astype(BF16).reshape(seg, CHUNK)
            et_ref[:, lane:lane + 1, :] = jnp.broadcast_to(jnp.exp(g_total), (K, 1, GDN_DV))

    sol = _unit_tri_solve(jnp.concatenate(systems, axis=0), jnp.concatenate(rhs_all, axis=0), on_diag, eye)
    for i, (u_ref, w_ref, hs) in enumerate(stores):
        part = sol[i * K:(i + 1) * K]
        u_ref[:, hs] = part[:, :, :GDN_DV].astype(BF16).reshape(seg, GDN_DV)
        w_ref[:, hs] = part[:, :, GDN_DV:].astype(BF16).reshape(seg, GDN_DK)


def _gdn_scan_kernel(uf_ref, wf_ref, qef_ref, kdf_ref, qkf_ref, etf_ref,
                     ub_ref, wb_ref, qeb_ref, kdb_ref, qkb_ref, etb_ref,
                     of_ref, ob_ref, st_ref):
    seg = uf_ref.shape[0]
    K = seg // CHUNK

    @pl.when(pl.program_id(1) == 0)
    def _():
        st_ref[...] = jnp.zeros(st_ref.shape, F32)

    dirs = ((uf_ref, wf_ref, qef_ref, kdf_ref, qkf_ref, etf_ref, of_ref),
            (ub_ref, wb_ref, qeb_ref, kdb_ref, qkb_ref, etb_ref, ob_ref))

    def step(t, carry):
        chains = []
        for d, (u_ref, w_ref, qe_ref, kd_ref, qk_ref, et_ref, o_ref) in enumerate(dirs):
            c = t if d == 0 else K - 1 - t
            rows = pl.ds(pl.multiple_of(c * CHUNK, CHUNK), CHUNK)
            for h in range(GDN_HEADS):
                chains.append((d * GDN_HEADS + h, h, c, rows, slice(h * GDN_DK, (h + 1) * GDN_DK),
                               u_ref, w_ref, qe_ref, kd_ref, qk_ref, et_ref, o_ref))
        sb = [st_ref[ch[0]].astype(BF16) for ch in chains]
        v_new = [u_ref[rows, hs] - _dot(w_ref[rows, hs], s)
                 for s, (_, _, _, rows, hs, u_ref, w_ref, *_) in zip(sb, chains)]
        o_inter = [_dot(qe_ref[rows, hs], s)
                   for s, (_, _, _, rows, hs, _, _, qe_ref, *_) in zip(sb, chains)]
        vb = [v.astype(BF16) for v in v_new]
        for v, oi, (lane, h, c, rows, hs, _, _, _, kd_ref, qk_ref, et_ref, o_ref) in zip(vb, o_inter, chains):
            o_ref[rows, hs] = (oi + _dot(qk_ref[h, rows, :], v)).astype(BF16)
            st_ref[lane] = st_ref[lane] * et_ref[c, lane:lane + 1, :] + _dot_tn(kd_ref[rows, hs], v)
        return carry

    lax.fori_loop(0, K, step, 0, unroll=True)


def _gdn(qc, kc, vc, gcol, grow, prep_seg, scan_seg):
    B, S, _ = qc.shape
    H = GDN_HEADS

    wide = lambda dt: jax.ShapeDtypeStruct((B, S, H * GDN_DK), dt)
    qk_shape = jax.ShapeDtypeStruct((B, H, S, CHUNK), BF16)
    et_shape = jax.ShapeDtypeStruct((B, S // CHUNK, 2 * H, GDN_DV), F32)
    dir_shapes = [wide(BF16), wide(BF16), wide(BF16), wide(BF16), qk_shape]
    here3 = lambda b, j: (b, j, 0)
    here4 = lambda b, j: (b, 0, j, 0)

    seg, K = prep_seg, prep_seg // CHUNK
    seg_spec = lambda imap: pl.BlockSpec((None, seg, H * GDN_DK), imap)
    qk_spec = lambda imap: pl.BlockSpec((None, H, seg, CHUNK), imap)
    et_spec = lambda imap: pl.BlockSpec((None, K, 2 * H, GDN_DV), imap)
    dir_specs = [seg_spec(here3)] * 4 + [qk_spec(here4)]
    prep = pl.pallas_call(
        _gdn_prep_kernel, grid=(B, S // seg),
        in_specs=[seg_spec(here3), seg_spec(here3), seg_spec(here3),
                  pl.BlockSpec((None, seg, HEAD_PAD), here3),
                  pl.BlockSpec((None, K, 4 * H, CHUNK), lambda b, j: (b, j, 0, 0))],
        out_specs=dir_specs + dir_specs + [et_spec(lambda b, j: (b, j, 0, 0))],
        out_shape=dir_shapes + dir_shapes + [et_shape],
        name="gdn_prep",
        compiler_params=pltpu.CompilerParams(dimension_semantics=("parallel", "parallel"),
                                             vmem_limit_bytes=VMEM_LIMIT_BYTES),
    )(qc, kc, vc, gcol, grow)
    fwd, bwd, et = prep[:5], prep[5:10], prep[10]

    seg, K = scan_seg, scan_seg // CHUNK
    nseg = S // seg
    seg_spec = lambda imap: pl.BlockSpec((None, seg, H * GDN_DK), imap)
    qk_spec = lambda imap: pl.BlockSpec((None, H, seg, CHUNK), imap)
    et_spec = lambda imap: pl.BlockSpec((None, K, 2 * H, GDN_DV), imap)
    back3 = lambda b, j: (b, nseg - 1 - j, 0)
    back4 = lambda b, j: (b, 0, nseg - 1 - j, 0)
    return pl.pallas_call(
        _gdn_scan_kernel, grid=(B, nseg),
        in_specs=[seg_spec(here3)] * 4 + [qk_spec(here4), et_spec(lambda b, j: (b, j, 0, 0))]
                 + [seg_spec(back3)] * 4 + [qk_spec(back4), et_spec(lambda b, j: (b, nseg - 1 - j, 0, 0))],
        out_specs=[seg_spec(here3), seg_spec(back3)],
        out_shape=[wide(BF16), wide(BF16)],
        scratch_shapes=[pltpu.VMEM((2 * H, GDN_DK, GDN_DV), F32)],
        name="gdn_scan",
        compiler_params=pltpu.CompilerParams(dimension_semantics=("parallel", "arbitrary"),
                                             vmem_limit_bytes=VMEM_LIMIT_BYTES),
    )(*fwd, et, *bwd, et)


def _out_kernel(x_ref, oa_ref, of_ref, ob_ref, gg_ref, sa_ref, sb_ref, gnw_ref, wpa_ref, wpb_ref, wo_ref,
                n2_ref, w1_ref, w2_ref, nf_ref, y_ref):
    parts = []
    for h in range(GDN_HEADS):
        hs = slice(h * GDN_DV, (h + 1) * GDN_DV)
        o = of_ref[:, hs].astype(F32) + ob_ref[:, hs].astype(F32)
        parts.append((_rms(o, gnw_ref[...]) * gg_ref[:, hs].astype(F32)).astype(BF16))
    o_b = jnp.concatenate(parts, axis=-1)

    merged = (sa_ref[...].astype(F32) * _dot(oa_ref[...], wpa_ref[...])
              + sb_ref[...].astype(F32) * _dot(o_b, wpb_ref[...]))
    x1 = x_ref[...] + _dot(merged.astype(BF16), wo_ref[...])
    h2 = _rms(x1, n2_ref[...]).astype(BF16)
    a = jnp.maximum(_dot(h2, w1_ref[...]), 0.0)
    x2 = x1 + _dot((a * a).astype(BF16), w2_ref[...])
    y_ref[...] = _rms(x2, nf_ref[...])


def _out_proj(x2, oa, o_f, o_b, gg, sa, sb, w, tm):
    T = x2.shape[0]

    def full(a):
        return pl.BlockSpec(a.shape, lambda i: (0,) * a.ndim, pipeline_mode=pl.Buffered(1))

    def rows(width):
        return pl.BlockSpec((tm, width), lambda i: (i, 0))

    consts = [w["gnw"], w["w_pa"], w["w_pb"], w["w_o"], w["n2"], w["w_1"], w["w_2"], w["nf"]]
    return pl.pallas_call(
        _out_kernel, grid=(T // tm,),
        in_specs=[rows(D_MODEL), rows(MLA_WIDTH), rows(GDN_WIDTH), rows(GDN_WIDTH), rows(GDN_WIDTH),
                  rows(D_MODEL), rows(D_MODEL)] + [full(a) for a in consts],
        out_specs=rows(D_MODEL),
        out_shape=jax.ShapeDtypeStruct((T, D_MODEL), F32),
        name="merge_mlp",
        compiler_params=pltpu.CompilerParams(dimension_semantics=("parallel",),
                                             vmem_limit_bytes=VMEM_LIMIT_BYTES),
    )(x2, oa, o_f, o_b, gg, sa, sb, *consts)


def _prepare_weights(norm1_w, w_in, q_norm_w, w_uq, kv_norm_w, w_ukv, a_log_f, dt_bias_f, a_log_b,
                     dt_bias_b, gdn_norm_w, w_proj_a, w_proj_b, w_out, norm2_w, w_ff1, w_ff2, final_norm_w):
    H = GDN_HEADS
    o_kv = MLA_Q_LORA
    o_kr = o_kv + MLA_KV_LORA
    o_gdn = o_kr + MLA_ROPE
    o_gab = o_gdn + GDN_QKV
    o_gg = o_gab + 4 * H
    o_ga = o_gg + GDN_WIDTH
    o_gb = o_ga + D_MODEL
    zeros = lambda n: jnp.zeros((D_MODEL, n), F32)
    w_kr = jnp.concatenate([zeros(MLA_NOPE), w_in[:, o_kr:o_gdn], zeros(HEAD_PAD - MLA_NOPE - MLA_ROPE)], 1)
    w_gab = w_in[:, o_gab:o_gg]
    w_small = jnp.concatenate([w_in[:, :o_kr], w_kr, w_gab, zeros(HEAD_PAD - 4 * H)], 1)

    dq = MLA_NOPE + MLA_ROPE
    w_uq_p = jnp.pad(w_uq.reshape(MLA_Q_LORA, MLA_HEADS, dq), ((0, 0), (0, 0), (0, HEAD_PAD - dq)))
    assert MLA_NOPE + MLA_V == HEAD_PAD and MLA_NOPE == MLA_V
    v_one = (jnp.arange(HEAD_PAD) == MLA_V).astype(F32).reshape(1, HEAD_PAD)

    inv_freq = 1.0 / (ROPE_THETA ** (jnp.arange(0, MLA_ROPE, 2, dtype=F32) / MLA_ROPE))
    z16 = jnp.zeros((ROPE_HALF,), F32)
    one16 = jnp.ones((ROPE_HALF,), F32)
    lane = lambda lo, hi: jnp.concatenate([jnp.zeros((MLA_NOPE,), F32), lo, hi,
                                           jnp.zeros((HEAD_PAD - MLA_NOPE - MLA_ROPE,), F32)])
    rope_c = jnp.stack([lane(inv_freq, inv_freq), lane(-one16, z16), lane(z16, one16)])

    bias = jnp.concatenate([dt_bias_f, dt_bias_b, jnp.zeros((2 * H,), F32)])
    neg_a = jnp.concatenate([-jnp.exp(a_log_f), -jnp.exp(a_log_b), jnp.zeros((2 * H,), F32)])
    g_row_c = jnp.stack([bias, neg_a], axis=1)
    g_col_c = jnp.pad(jnp.stack([bias, neg_a]), ((0, 0), (0, HEAD_PAD - 4 * H)))

    return {
        "n1": norm1_w.reshape(1, D_MODEL),
        "w_small": w_small.astype(BF16),
        "w_gdn": w_in[:, o_gdn:o_gab].astype(BF16),
        "w_gg": w_in[:, o_gg:o_ga].astype(BF16),
        "w_ga": w_in[:, o_ga:o_gb].astype(BF16),
        "w_gb": w_in[:, o_gb:].astype(BF16),
        "w_gt": w_gab.T.astype(BF16),
        "qn": q_norm_w.reshape(1, MLA_Q_LORA),
        "w_uq": w_uq_p.reshape(MLA_Q_LORA, MLA_HEADS * HEAD_PAD).astype(BF16),
        "kvn": kv_norm_w.reshape(1, MLA_KV_LORA),
        "w_ukv": w_ukv.astype(BF16),
        "v_one": v_one,
        "rope_c": rope_c,
        "g_col_c": g_col_c,
        "g_row_c": g_row_c,
        "gnw": gdn_norm_w.reshape(1, GDN_DV),
        "w_pa": w_proj_a.astype(BF16),
        "w_pb": w_proj_b.astype(BF16),
        "w_o": w_out.astype(BF16),
        "n2": norm2_w.reshape(1, D_MODEL),
        "w_1": w_ff1.astype(BF16),
        "w_2": w_ff2.astype(BF16),
        "nf": final_norm_w.reshape(1, D_MODEL),
    }


def _tile_sizes(S):
    return dict(
        tm=min(512, S),
        tq=min(512, S),
        prep_seg=min(512, S),
        scan_seg=min(1024, S),
    )


def _layer(x, positions, conv_w, w):
    B, S, _ = x.shape
    T = B * S
    ts = _tile_sizes(S)
    tm = ts["tm"]
    x2 = x.reshape(T, D_MODEL)
    q, k, v, zg, gg, sa, sb, gcol, grow = _in_proj(x2, positions.reshape(T, 1), w, tm)

    o_a, qc, kc, vc = _attention(q.reshape(B, S, -1), k.reshape(B, S, -1), v.reshape(B, S, -1),
                                 zg.reshape(B, S, GDN_QKV), conv_w, ts["tq"])

    grow_c = grow.reshape(4 * GDN_HEADS, B, S // CHUNK, CHUNK).transpose(1, 2, 0, 3)
    o_f, o_b = _gdn(qc, kc, vc, gcol.reshape(B, S, HEAD_PAD), grow_c, ts["prep_seg"], ts["scan_seg"])

    y = _out_proj(x2, o_a.reshape(T, MLA_WIDTH), o_f.reshape(T, GDN_WIDTH), o_b.reshape(T, GDN_WIDTH),
                  gg, sa, sb, w, tm)
    return y.reshape(B, S, D_MODEL)


def kernel(x, positions, norm1_w, w_in, q_norm_w, w_uq, kv_norm_w, w_ukv, conv_w, a_log_f, dt_bias_f,
           a_log_b, dt_bias_b, gdn_norm_w, w_proj_a, w_proj_b, w_out, norm2_w, w_ff1, w_ff2,
           final_norm_w):
    assert norm1_w.shape[0] == 1, "single-layer block"
    w = _prepare_weights(norm1_w[0], w_in[0], q_norm_w[0], w_uq[0], kv_norm_w[0], w_ukv[0], a_log_f[0],
                         dt_bias_f[0], a_log_b[0], dt_bias_b[0], gdn_norm_w[0], w_proj_a[0], w_proj_b[0],
                         w_out[0], norm2_w[0], w_ff1[0], w_ff2[0], final_norm_w)
    return _layer(x, positions, conv_w[0], w)
```

```python
import functools
import math

import jax
import jax.numpy as jnp
from jax import lax
from jax.experimental import pallas as pl
from jax.experimental.pallas import tpu as pltpu

F32 = jnp.float32
BF16 = jnp.bfloat16

D_MODEL = 1024
MLA_HEADS = 8
MLA_Q_LORA = 256
MLA_KV_LORA = 128
MLA_NOPE = 64
MLA_ROPE = 32
MLA_V = 64
ROPE_THETA = 10000.0
GDN_HEADS = 4
GDN_DK = 128
GDN_DV = 128
GDN_CONV = 5
CHUNK = 64
SUB = 16
D_FF = 4 * D_MODEL
EPS = 1e-6

MLA_WIDTH = MLA_HEADS * MLA_V
GDN_QK = GDN_HEADS * GDN_DK
GDN_WIDTH = GDN_HEADS * GDN_DV
GDN_QKV = 2 * GDN_QK + GDN_WIDTH
HEAD_PAD = 128
ROPE_HALF = MLA_ROPE // 2

VMEM_LIMIT_BYTES = 56 * 1024 * 1024


def _dot(a, b):
    return jnp.dot(a, b, preferred_element_type=F32)


def _dot_nt(a, b):
    return lax.dot_general(a, b, (((1,), (1,)), ((), ())), preferred_element_type=F32)


def _dot_tn(a, b):
    return lax.dot_general(a, b, (((0,), (0,)), ((), ())), preferred_element_type=F32)


def _sigmoid(x):
    return 1.0 / (1.0 + jnp.exp(-x))


def _silu(x):
    return x * _sigmoid(x)


def _softplus(x):
    return jnp.maximum(x, 0.0) + jnp.log(1.0 + jnp.exp(-jnp.abs(x)))


def _rms(x, w):
    return x * lax.rsqrt(jnp.mean(x * x, axis=-1, keepdims=True) + EPS) * w


def _chunk_cumsum(x, axis, reverse):
    n = x.shape[axis]
    pos = lax.broadcasted_iota(jnp.int32, x.shape, axis) % CHUNK
    s = 1
    while s < CHUNK:
        if reverse:
            shifted = pltpu.roll(x, n - s, axis)
            x = x + jnp.where(pos < CHUNK - s, shifted, 0.0)
        else:
            shifted = pltpu.roll(x, s, axis)
            x = x + jnp.where(pos >= s, shifted, 0.0)
        s *= 2
    return x


def _decay_beta(z, bias, neg_a, axis):
    idx = lax.broadcasted_iota(jnp.int32, z.shape, 1 - axis)
    g = neg_a * _softplus(z + bias)
    g_f = _chunk_cumsum(jnp.where(idx < GDN_HEADS, g, 0.0), axis, reverse=False)
    g_b = _chunk_cumsum(jnp.where((idx >= GDN_HEADS) & (idx < 2 * GDN_HEADS), g, 0.0), axis, reverse=True)
    return jnp.where(idx < GDN_HEADS, g_f, jnp.where(idx < 2 * GDN_HEADS, g_b, _sigmoid(z)))


def _rope(t, cos, sin_lo, sin_hi):
    return (t * cos + pltpu.roll(t, HEAD_PAD - ROPE_HALF, 1) * sin_lo
            + pltpu.roll(t, ROPE_HALF, 1) * sin_hi)


def _in_proj_kernel(x_ref, pos_ref, n1_ref, wsm_ref, wgdn_ref, wgg_ref, wga_ref, wgb_ref, wgt_ref,
                    qn_ref, wuq_ref, kvn_ref, wukv_ref, vone_ref, ropec_ref, gcc_ref, gcr_ref,
                    q_out, k_out, v_out, zg_out, gg_out, sa_out, sb_out, gcol_out, grow_out):
    hb = _rms(x_ref[...], n1_ref[...]).astype(BF16)

    zs = _dot(hb, wsm_ref[...])
    c_q = zs[:, :MLA_Q_LORA]
    c_kv = zs[:, MLA_Q_LORA:MLA_Q_LORA + MLA_KV_LORA]
    k_r = zs[:, 384:512]
    z_gab = zs[:, 512:640]

    col = _decay_beta(z_gab, gcc_ref[0:1, :], gcc_ref[1:2, :], axis=0)
    gcol_out[...] = col
    z_t = _dot_nt(wgt_ref[...], hb)
    grow_out[...] = _decay_beta(z_t, gcr_ref[:, 0:1], gcr_ref[:, 1:2], axis=1)

    zg_out[...] = _dot(hb, wgdn_ref[...]).astype(BF16)
    gg_out[...] = _silu(_dot(hb, wgg_ref[...])).astype(BF16)
    sa_out[...] = _sigmoid(_dot(hb, wga_ref[...])).astype(BF16)
    sb_out[...] = _sigmoid(_dot(hb, wgb_ref[...])).astype(BF16)

    ang = pos_ref[...].astype(F32) * ropec_ref[0:1, :]
    cos = jnp.cos(ang)
    sin = jnp.sin(ang)
    sin_lo = sin * ropec_ref[1:2, :]
    sin_hi = sin * ropec_ref[2:3, :]

    qf = _dot(_rms(c_q, qn_ref[...]).astype(BF16), wuq_ref[...])
    scale = math.log2(math.e) / math.sqrt(MLA_NOPE + MLA_ROPE)
    for h in range(MLA_HEADS):
        sl = slice(h * HEAD_PAD, (h + 1) * HEAD_PAD)
        q_out[:, sl] = (_rope(qf[:, sl], cos, sin_lo, sin_hi) * scale).astype(BF16)

    ckv = _rms(c_kv, kvn_ref[...]).astype(BF16)
    kv = _dot(ckv, wukv_ref[...])
    kr = _rope(k_r, cos, sin_lo, sin_hi)
    low = lax.broadcasted_iota(jnp.int32, (1, HEAD_PAD), 1) < MLA_NOPE
    for h in range(MLA_HEADS):
        sl = slice(h * HEAD_PAD, (h + 1) * HEAD_PAD)
        k_out[:, sl] = jnp.where(low, kv[:, sl], kr).astype(BF16)
        v_out[:, sl] = jnp.where(low, pltpu.roll(kv[:, sl], MLA_V, 1), vone_ref[...]).astype(BF16)


def _in_proj(x2, pos2, w, tm):
    T = x2.shape[0]
    grid = (T // tm,)

    def full(a):
        return pl.BlockSpec(a.shape, lambda i: (0,) * a.ndim, pipeline_mode=pl.Buffered(1))

    def rows(width):
        return pl.BlockSpec((tm, width), lambda i: (i, 0))

    ins = [x2, pos2, w["n1"], w["w_small"], w["w_gdn"], w["w_gg"], w["w_ga"], w["w_gb"], w["w_gt"],
           w["qn"], w["w_uq"], w["kvn"], w["w_ukv"], w["v_one"], w["rope_c"], w["g_col_c"], w["g_row_c"]]
    in_specs = [rows(D_MODEL), rows(1)] + [full(a) for a in ins[2:]]
    out_shape = [
        jax.ShapeDtypeStruct((T, MLA_HEADS * HEAD_PAD), BF16),
        jax.ShapeDtypeStruct((T, MLA_HEADS * HEAD_PAD), BF16),
        jax.ShapeDtypeStruct((T, MLA_HEADS * HEAD_PAD), BF16),
        jax.ShapeDtypeStruct((T, GDN_QKV), BF16),
        jax.ShapeDtypeStruct((T, GDN_WIDTH), BF16),
        jax.ShapeDtypeStruct((T, D_MODEL), BF16),
        jax.ShapeDtypeStruct((T, D_MODEL), BF16),
        jax.ShapeDtypeStruct((T, HEAD_PAD), F32),
        jax.ShapeDtypeStruct((4 * GDN_HEADS, T), F32),
    ]
    out_specs = [rows(MLA_HEADS * HEAD_PAD), rows(MLA_HEADS * HEAD_PAD), rows(MLA_HEADS * HEAD_PAD), rows(GDN_QKV),
                 rows(GDN_WIDTH), rows(D_MODEL), rows(D_MODEL), rows(HEAD_PAD),
                 pl.BlockSpec((4 * GDN_HEADS, tm), lambda i: (0, i))]
    return pl.pallas_call(
        _in_proj_kernel, grid=grid, in_specs=in_specs, out_specs=out_specs, out_shape=out_shape,
        name="in_proj",
        compiler_params=pltpu.CompilerParams(dimension_semantics=("parallel",),
                                             vmem_limit_bytes=VMEM_LIMIT_BYTES),
    )(*ins)


ATTN_GROUP = 2
ATTN_Q_TILES = 4
GDN_HALO = 16


def _attn_kernel(q_ref, k_ref, v_ref, *rest, tq):
    z_refs, zp_refs, zn_refs, cw_refs = rest[0:3], rest[3:6], rest[6:9], rest[9:12]
    o_ref, conv_out_refs, pad_s = rest[12], rest[13:16], rest[16]
    R = q_ref.shape[0]
    it = pl.program_id(2)
    halo = GDN_HALO

    for s, (z_ref, zp_ref, zn_ref) in enumerate(zip(z_refs, zp_refs, zn_refs)):
        pad_s[s, 0:halo, :] = jnp.where(it == 0, 0.0, zp_ref[...].astype(F32))
        pad_s[s, halo:halo + R, :] = z_ref[...].astype(F32)
        pad_s[s, halo + R:, :] = jnp.where(it == pl.num_programs(2) - 1, 0.0, zn_ref[...].astype(F32))

    stream_norm = ((True, GDN_DK ** -0.5), (True, 1.0), (False, 1.0))

    for i in range(R // tq):
        rows = slice(i * tq, (i + 1) * tq)
        for s, (cw_ref, out_ref, (l2, mult)) in enumerate(zip(cw_refs, conv_out_refs, stream_norm)):
            acc = jnp.zeros((tq, GDN_DK), F32)
            for t in range(GDN_CONV):
                off = halo + i * tq + t - GDN_CONV // 2
                acc = acc + pad_s[s, off:off + tq, :] * cw_ref[t:t + 1, :]
            y = _silu(acc)
            if l2:
                y = y * (lax.rsqrt(jnp.sum(y * y, axis=-1, keepdims=True) + EPS) * mult)
            out_ref[rows, :] = y.astype(BF16)

        scores = [_dot_nt(q_ref[rows, j * HEAD_PAD:(j + 1) * HEAD_PAD], k_ref[:, j * HEAD_PAD:(j + 1) * HEAD_PAD])
                  for j in range(ATTN_GROUP)]

        outs = []
        for j, s in enumerate(scores):
            p = jnp.exp2((s - jnp.max(s, axis=-1, keepdims=True)).astype(BF16))
            o = _dot(p, v_ref[:, j * HEAD_PAD:(j + 1) * HEAD_PAD])
            outs.append(o[:, :MLA_V] / o[:, MLA_V:MLA_V + 1])
        o_ref[rows, :] = jnp.concatenate(outs, axis=-1).astype(BF16)


def _attention(q, k, v, zg, conv_w, tq):
    B, S, _ = q.shape
    H = GDN_HEADS
    assert MLA_HEADS // ATTN_GROUP == H
    R = min(ATTN_Q_TILES * tq, S)
    hb = R // GDN_HALO
    last_hb = S // GDN_HALO - 1
    kv = pl.BlockSpec((None, S, ATTN_GROUP * HEAD_PAD), lambda b, g, r: (b, 0, g))
    z_main = [pl.BlockSpec((None, R, GDN_DK), lambda b, g, r, s=s: (b, r, s * H + g)) for s in range(3)]
    z_prev = [pl.BlockSpec((None, GDN_HALO, GDN_DK), lambda b, g, r, s=s: (b, jnp.maximum(r * hb - 1, 0), s * H + g))
              for s in range(3)]
    z_next = [pl.BlockSpec((None, GDN_HALO, GDN_DK),
                           lambda b, g, r, s=s: (b, jnp.minimum((r + 1) * hb, last_hb), s * H + g)) for s in range(3)]
    cw = [pl.BlockSpec((GDN_CONV, GDN_DK), lambda b, g, r, s=s: (0, s * H + g)) for s in range(3)]
    slab_out = pl.BlockSpec((None, R, GDN_DK), lambda b, g, r: (b, r, g))
    slab_shape = jax.ShapeDtypeStruct((B, S, H * GDN_DK), BF16)
    return pl.pallas_call(
        functools.partial(_attn_kernel, tq=tq), grid=(B, MLA_HEADS // ATTN_GROUP, S // R),
        in_specs=[pl.BlockSpec((None, R, ATTN_GROUP * HEAD_PAD), lambda b, g, r: (b, r, g)), kv, kv]
                 + z_main + z_prev + z_next + cw,
        out_specs=[pl.BlockSpec((None, R, ATTN_GROUP * MLA_V), lambda b, g, r: (b, r, g)),
                   slab_out, slab_out, slab_out],
        out_shape=[jax.ShapeDtypeStruct((B, S, MLA_WIDTH), BF16), slab_shape, slab_shape, slab_shape],
        scratch_shapes=[pltpu.VMEM((3, R + 2 * GDN_HALO, GDN_DK), F32)],
        name="mla_attention",
        compiler_params=pltpu.CompilerParams(dimension_semantics=("parallel", "parallel", "parallel"),
                                             vmem_limit_bytes=VMEM_LIMIT_BYTES),
    )(q, k, v, *([zg] * 9), *([conv_w] * 3))


def _bmm(a, b):
    return jnp.einsum("kij,kjl->kil", a, b, preferred_element_type=F32)


def _bmm_nt(a, b):
    return jnp.einsum("kid,kjd->kij", a, b, preferred_element_type=F32)


def _unit_tri_solve(l_mat, rhs, on_diag, eye):
    n1 = jnp.where(on_diag, -l_mat, 0.0)
    l_off = jnp.where(on_diag, 0.0, l_mat)
    n2 = _bmm(n1, n1)
    d = eye + n1
    d = d + _bmm(d, n2)
    n4 = _bmm(n2, n2)
    d = d + _bmm(d, n4)
    n8 = _bmm(n4, n4)
    d = d + _bmm(d, n8)
    db = d.astype(BF16)
    m = _bmm(db, l_off.astype(BF16))
    y0 = _bmm(db, rhs.astype(BF16))
    mb = m.astype(BF16)
    m2 = _bmm(mb, mb)
    z1 = y0 - _bmm(mb, y0.astype(BF16))
    return z1 + _bmm(m2.astype(BF16), z1.astype(BF16))


def _gdn_prep_kernel(qc_ref, kc_ref, vc_ref, gcol_ref, grow_ref,
                     uf_ref, wf_ref, qef_ref, kdf_ref, qkf_ref,
                     ub_ref, wb_ref, qeb_ref, kdb_ref, qkb_ref, et_ref):
    seg = qc_ref.shape[0]
    K = seg // CHUNK

    row = lax.broadcasted_iota(jnp.int32, (CHUNK, CHUNK), 0)
    col = lax.broadcasted_iota(jnp.int32, (CHUNK, CHUNK), 1)
    on_diag = (row // SUB) == (col // SUB)
    eye = (row == col).astype(F32)
    gc = gcol_ref[...].reshape(K, CHUNK, HEAD_PAD)

    systems, rhs_all, stores = [], [], []
    for h in range(GDN_HEADS):
        hs = slice(h * GDN_DK, (h + 1) * GDN_DK)
        qb = qc_ref[:, hs].reshape(K, CHUNK, GDN_DK)
        kb = kc_ref[:, hs].reshape(K, CHUNK, GDN_DK)
        q3, k3, v3 = qb.astype(F32), kb.astype(F32), vc_ref[:, hs].astype(F32).reshape(K, CHUNK, GDN_DV)
        gram = _bmm_nt(kb, kb)
        qk = _bmm_nt(qb, kb)
        for d, keep_incl, keep_strict, total_row, refs in (
                (0, row >= col, row > col, CHUNK - 1, (uf_ref, wf_ref, qef_ref, kdf_ref, qkf_ref)),
                (1, row <= col, row < col, 0, (ub_ref, wb_ref, qeb_ref, kdb_ref, qkb_ref))):
            u_ref, w_ref, qe_ref, kd_ref, qk_ref = refs
            lane = d * GDN_HEADS + h
            g_col = gc[:, :, lane:lane + 1]
            g_row = grow_ref[:, lane:lane + 1, :]
            beta = gc[:, :, 2 * GDN_HEADS + lane:2 * GDN_HEADS + lane + 1]
            decay = jnp.where(keep_incl, jnp.exp(jnp.where(keep_incl, g_col - g_row, 0.0)), 0.0)
            e_g = jnp.exp(g_col)
            g_total = g_col[:, total_row:total_row + 1, :]
            systems.append(jnp.where(keep_strict, gram * beta * decay, 0.0))
            rhs_all.append(jnp.concatenate([v3 * beta, k3 * (beta * e_g)], axis=-1))
            stores.append((u_ref, w_ref, hs))
            qe_ref[:, hs] = (q3 * e_g).astype(BF16).reshape(seg, GDN_DK)
            kd_ref[:, hs] = (k3 * jnp.exp(g_total - g_col)).astype(BF16).reshape(seg, GDN_DK)
            qk_ref[h] = (qk * decay).astype(BF16).reshape(seg, CHUNK)
            et_ref[:, lane:lane + 1, :] = jnp.broadcast_to(jnp.exp(g_total), (K, 1, GDN_DV))

    sol = _unit_tri_solve(jnp.concatenate(systems, axis=0), jnp.concatenate(rhs_all, axis=0), on_diag, eye)
    for i, (u_ref, w_ref, hs) in enumerate(stores):
        part = sol[i * K:(i + 1) * K]
        u_ref[:, hs] = part[:, :, :GDN_DV].astype(BF16).reshape(seg, GDN_DV)
        w_ref[:, hs] = part[:, :, GDN_DV:].astype(BF16).reshape(seg, GDN_DK)


def _gdn_scan_kernel(uf_ref, wf_ref, qef_ref, kdf_ref, qkf_ref, etf_ref,
                     ub_ref, wb_ref, qeb_ref, kdb_ref, qkb_ref, etb_ref,
                     of_ref, ob_ref, st_ref):
    seg = uf_ref.shape[0]
    K = seg // CHUNK

    @pl.when(pl.program_id(1) == 0)
    def _():
        st_ref[...] = jnp.zeros(st_ref.shape, F32)

    dirs = ((uf_ref, wf_ref, qef_ref, kdf_ref, qkf_ref, etf_ref, of_ref),
            (ub_ref, wb_ref, qeb_ref, kdb_ref, qkb_ref, etb_ref, ob_ref))

    def step(t, carry):
        chains = []
        for d, (u_ref, w_ref, qe_ref, kd_ref, qk_ref, et_ref, o_ref) in enumerate(dirs):
            c = t if d == 0 else K - 1 - t
            rows = pl.ds(pl.multiple_of(c * CHUNK, CHUNK), CHUNK)
            for h in range(GDN_HEADS):
                chains.append((d * GDN_HEADS + h, h, c, rows, slice(h * GDN_DK, (h + 1) * GDN_DK),
                               u_ref, w_ref, qe_ref, kd_ref, qk_ref, et_ref, o_ref))
        sb = [st_ref[ch[0]].astype(BF16) for ch in chains]
        v_new = [u_ref[rows, hs] - _dot(w_ref[rows, hs], s)
                 for s, (_, _, _, rows, hs, u_ref, w_ref, *_) in zip(sb, chains)]
        o_inter = [_dot(qe_ref[rows, hs], s)
                   for s, (_, _, _, rows, hs, _, _, qe_ref, *_) in zip(sb, chains)]
        vb = [v.astype(BF16) for v in v_new]
        for v, oi, (lane, h, c, rows, hs, _, _, _, kd_ref, qk_ref, et_ref, o_ref) in zip(vb, o_inter, chains):
            o_ref[rows, hs] = (oi + _dot(qk_ref[h, rows, :], v)).astype(BF16)
            st_ref[lane] = st_ref[lane] * et_ref[c, lane:lane + 1, :] + _dot_tn(kd_ref[rows, hs], v)
        return carry

    lax.fori_loop(0, K, step, 0, unroll=True)


def _gdn(qc, kc, vc, gcol, grow, prep_seg, scan_seg):
    B, S, _ = qc.shape
    H = GDN_HEADS

    wide = lambda dt: jax.ShapeDtypeStruct((B, S, H * GDN_DK), dt)
    qk_shape = jax.ShapeDtypeStruct((B, H, S, CHUNK), BF16)
    et_shape = jax.ShapeDtypeStruct((B, S // CHUNK, 2 * H, GDN_DV), F32)
    dir_shapes = [wide(BF16), wide(BF16), wide(BF16), wide(BF16), qk_shape]
    here3 = lambda b, j: (b, j, 0)
    here4 = lambda b, j: (b, 0, j, 0)

    seg, K = prep_seg, prep_seg // CHUNK
    seg_spec = lambda imap: pl.BlockSpec((None, seg, H * GDN_DK), imap)
    qk_spec = lambda imap: pl.BlockSpec((None, H, seg, CHUNK), imap)
    et_spec = lambda imap: pl.BlockSpec((None, K, 2 * H, GDN_DV), imap)
    dir_specs = [seg_spec(here3)] * 4 + [qk_spec(here4)]
    prep = pl.pallas_call(
        _gdn_prep_kernel, grid=(B, S // seg),
        in_specs=[seg_spec(here3), seg_spec(here3), seg_spec(here3),
                  pl.BlockSpec((None, seg, HEAD_PAD), here3),
                  pl.BlockSpec((None, K, 4 * H, CHUNK), lambda b, j: (b, j, 0, 0))],
        out_specs=dir_specs + dir_specs + [et_spec(lambda b, j: (b, j, 0, 0))],
        out_shape=dir_shapes + dir_shapes + [et_shape],
        name="gdn_prep",
        compiler_params=pltpu.CompilerParams(dimension_semantics=("parallel", "parallel"),
                                             vmem_limit_bytes=VMEM_LIMIT_BYTES),
    )(qc, kc, vc, gcol, grow)
    fwd, bwd, et = prep[:5], prep[5:10], prep[10]

    seg, K = scan_seg, scan_seg // CHUNK
    nseg = S // seg
    seg_spec = lambda imap: pl.BlockSpec((None, seg, H * GDN_DK), imap)
    qk_spec = lambda imap: pl.BlockSpec((None, H, seg, CHUNK), imap)
    et_spec = lambda imap: pl.BlockSpec((None, K, 2 * H, GDN_DV), imap)
    back3 = lambda b, j: (b, nseg - 1 - j, 0)
    back4 = lambda b, j: (b, 0, nseg - 1 - j, 0)
    return pl.pallas_call(
        _gdn_scan_kernel, grid=(B, nseg),
        in_specs=[seg_spec(here3)] * 4 + [qk_spec(here4), et_spec(lambda b, j: (b, j, 0, 0))]
                 + [seg_spec(back3)] * 4 + [qk_spec(back4), et_spec(lambda b, j: (b, nseg - 1 - j, 0, 0))],
        out_specs=[seg_spec(here3), seg_spec(back3)],
        out_shape=[wide(BF16), wide(BF16)],
        scratch_shapes=[pltpu.VMEM((2 * H, GDN_DK, GDN_DV), F32)],
        name="gdn_scan",
        compiler_params=pltpu.CompilerParams(dimension_semantics=("parallel", "arbitrary"),
                                             vmem_limit_bytes=VMEM_LIMIT_BYTES),
    )(*fwd, et, *bwd, et)


def _out_kernel(x_ref, oa_ref, of_ref, ob_ref, gg_ref, sa_ref, sb_ref, gnw_ref, wpa_ref, wpb_ref, wo_ref,
                n2_ref, w1_ref, w2_ref, nf_ref, y_ref):
    parts = []
    for h in range(GDN_HEADS):
        hs = slice(h * GDN_DV, (h + 1) * GDN_DV)
        o = of_ref[:, hs].astype(F32) + ob_ref[:, hs].astype(F32)
        parts.append((_rms(o, gnw_ref[...]) * gg_ref[:, hs].astype(F32)).astype(BF16))
    o_b = jnp.concatenate(parts, axis=-1)

    merged = (sa_ref[...].astype(F32) * _dot(oa_ref[...], wpa_ref[...])
              + sb_ref[...].astype(F32) * _dot(o_b, wpb_ref[...]))
    x1 = x_ref[...] + _dot(merged.astype(BF16), wo_ref[...])
    h2 = _rms(x1, n2_ref[...]).astype(BF16)
    a = jnp.maximum(_dot(h2, w1_ref[...]), 0.0)
    x2 = x1 + _dot((a * a).astype(BF16), w2_ref[...])
    y_ref[...] = _rms(x2, nf_ref[...])


def _out_proj(x2, oa, o_f, o_b, gg, sa, sb, w, tm):
    T = x2.shape[0]

    def full(a):
        return pl.BlockSpec(a.shape, lambda i: (0,) * a.ndim, pipeline_mode=pl.Buffered(1))

    def rows(width):
        return pl.BlockSpec((tm, width), lambda i: (i, 0))

    consts = [w["gnw"], w["w_pa"], w["w_pb"], w["w_o"], w["n2"], w["w_1"], w["w_2"], w["nf"]]
    return pl.pallas_call(
        _out_kernel, grid=(T // tm,),
        in_specs=[rows(D_MODEL), rows(MLA_WIDTH), rows(GDN_WIDTH), rows(GDN_WIDTH), rows(GDN_WIDTH),
                  rows(D_MODEL), rows(D_MODEL)] + [full(a) for a in consts],
        out_specs=rows(D_MODEL),
        out_shape=jax.ShapeDtypeStruct((T, D_MODEL), F32),
        name="merge_mlp",
        compiler_params=pltpu.CompilerParams(dimension_semantics=("parallel",),
                                             vmem_limit_bytes=VMEM_LIMIT_BYTES),
    )(x2, oa, o_f, o_b, gg, sa, sb, *consts)


def _prepare_weights(norm1_w, w_in, q_norm_w, w_uq, kv_norm_w, w_ukv, a_log_f, dt_bias_f, a_log_b,
                     dt_bias_b, gdn_norm_w, w_proj_a, w_proj_b, w_out, norm2_w, w_ff1, w_ff2, final_norm_w):
    H = GDN_HEADS
    o_kv = MLA_Q_LORA
    o_kr = o_kv + MLA_KV_LORA
    o_gdn = o_kr + MLA_ROPE
    o_gab = o_gdn + GDN_QKV
    o_gg = o_gab + 4 * H
    o_ga = o_gg + GDN_WIDTH
    o_gb = o_ga + D_MODEL
    zeros = lambda n: jnp.zeros((D_MODEL, n), F32)
    w_kr = jnp.concatenate([zeros(MLA_NOPE), w_in[:, o_kr:o_gdn], zeros(HEAD_PAD - MLA_NOPE - MLA_ROPE)], 1)
    w_gab = w_in[:, o_gab:o_gg]
    w_small = jnp.concatenate([w_in[:, :o_kr], w_kr, w_gab, zeros(HEAD_PAD - 4 * H)], 1)

    dq = MLA_NOPE + MLA_ROPE
    w_uq_p = jnp.pad(w_uq.reshape(MLA_Q_LORA, MLA_HEADS, dq), ((0, 0), (0, 0), (0, HEAD_PAD - dq)))
    assert MLA_NOPE + MLA_V == HEAD_PAD and MLA_NOPE == MLA_V
    v_one = (jnp.arange(HEAD_PAD) == MLA_V).astype(F32).reshape(1, HEAD_PAD)

    inv_freq = 1.0 / (ROPE_THETA ** (jnp.arange(0, MLA_ROPE, 2, dtype=F32) / MLA_ROPE))
    z16 = jnp.zeros((ROPE_HALF,), F32)
    one16 = jnp.ones((ROPE_HALF,), F32)
    lane = lambda lo, hi: jnp.concatenate([jnp.zeros((MLA_NOPE,), F32), lo, hi,
                                           jnp.zeros((HEAD_PAD - MLA_NOPE - MLA_ROPE,), F32)])
    rope_c = jnp.stack([lane(inv_freq, inv_freq), lane(-one16, z16), lane(z16, one16)])

    bias = jnp.concatenate([dt_bias_f, dt_bias_b, jnp.zeros((2 * H,), F32)])
    neg_a = jnp.concatenate([-jnp.exp(a_log_f), -jnp.exp(a_log_b), jnp.zeros((2 * H,), F32)])
    g_row_c = jnp.stack([bias, neg_a], axis=1)
    g_col_c = jnp.pad(jnp.stack([bias, neg_a]), ((0, 0), (0, HEAD_PAD - 4 * H)))

    return {
        "n1": norm1_w.reshape(1, D_MODEL),
        "w_small": w_small.astype(BF16),
        "w_gdn": w_in[:, o_gdn:o_gab].astype(BF16),
        "w_gg": w_in[:, o_gg:o_ga].astype(BF16),
        "w_ga": w_in[:, o_ga:o_gb].astype(BF16),
        "w_gb": w_in[:, o_gb:].astype(BF16),
        "w_gt": w_gab.T.astype(BF16),
        "qn": q_norm_w.reshape(1, MLA_Q_LORA),
        "w_uq": w_uq_p.reshape(MLA_Q_LORA, MLA_HEADS * HEAD_PAD).astype(BF16),
        "kvn": kv_norm_w.reshape(1, MLA_KV_LORA),
        "w_ukv": w_ukv.astype(BF16),
        "v_one": v_one,
        "rope_c": rope_c,
        "g_col_c": g_col_c,
        "g_row_c": g_row_c,
        "gnw": gdn_norm_w.reshape(1, GDN_DV),
        "w_pa": w_proj_a.astype(BF16),
        "w_pb": w_proj_b.astype(BF16),
        "w_o": w_out.astype(BF16),
        "n2": norm2_w.reshape(1, D_MODEL),
        "w_1": w_ff1.astype(BF16),
        "w_2": w_ff2.astype(BF16),
        "nf": final_norm_w.reshape(1, D_MODEL),
    }


def _tile_sizes(S):
    return dict(
        tm=min(512, S),
        tq=min(512, S),
        prep_seg=min(512, S),
        scan_seg=min(1024, S),
    )


def _layer(x, positions, conv_w, w):
    B, S, _ = x.shape
    T = B * S
    ts = _tile_sizes(S)
    tm = ts["tm"]
    x2 = x.reshape(T, D_MODEL)
    q, k, v, zg, gg, sa, sb, gcol, grow = _in_proj(x2, positions.reshape(T, 1), w, tm)

    o_a, qc, kc, vc = _attention(q.reshape(B, S, -1), k.reshape(B, S, -1), v.reshape(B, S, -1),
                                 zg.reshape(B, S, GDN_QKV), conv_w, ts["tq"])

    grow_c = grow.reshape(4 * GDN_HEADS, B, S // CHUNK, CHUNK).transpose(1, 2, 0, 3)
    o_f, o_b = _gdn(qc, kc, vc, gcol.reshape(B, S, HEAD_PAD), grow_c, ts["prep_seg"], ts["scan_seg"])

    y = _out_proj(x2, o_a.reshape(T, MLA_WIDTH), o_f.reshape(T, GDN_WIDTH), o_b.reshape(T, GDN_WIDTH),
                  gg, sa, sb, w, tm)
    return y.reshape(B, S, D_MODEL)


def kernel(x, positions, norm1_w, w_in, q_norm_w, w_uq, kv_norm_w, w_ukv, conv_w, a_log_f, dt_bias_f,
           a_log_b, dt_bias_b, gdn_norm_w, w_proj_a, w_proj_b, w_out, norm2_w, w_ff1, w_ff2,
           final_norm_w):
    assert norm1_w.shape[0] == 1, "single-layer block"
    w = _prepare_weights(norm1_w[0], w_in[0], q_norm_w[0], w_uq[0], kv_norm_w[0], w_ukv[0], a_log_f[0],
                         dt_bias_f[0], a_log_b[0], dt_bias_b[0], gdn_norm_w[0], w_proj_a[0], w_proj_b[0],
                         w_out[0], norm2_w[0], w_ff1[0], w_ff2[0], final_norm_w)
    return _layer(x, positions, conv_w[0], w)
```

```python
import functools
import math

import jax
import jax.numpy as jnp
from jax import lax
from jax.experimental import pallas as pl
from jax.experimental.pallas import tpu as pltpu

F32 = jnp.float32
BF16 = jnp.bfloat16

D_MODEL = 1024
MLA_HEADS = 8
MLA_Q_LORA = 256
MLA_KV_LORA = 128
MLA_NOPE = 64
MLA_ROPE = 32
MLA_V = 64
ROPE_THETA = 10000.0
GDN_HEADS = 4
GDN_DK = 128
GDN_DV = 128
GDN_CONV = 5
CHUNK = 64
SUB = 16
D_FF = 4 * D_MODEL
EPS = 1e-6

MLA_WIDTH = MLA_HEADS * MLA_V
GDN_QK = GDN_HEADS * GDN_DK
GDN_WIDTH = GDN_HEADS * GDN_DV
GDN_QKV = 2 * GDN_QK + GDN_WIDTH
HEAD_PAD = 128
ROPE_HALF = MLA_ROPE // 2

VMEM_LIMIT_BYTES = 56 * 1024 * 1024


def _dot(a, b):
    return jnp.dot(a, b, preferred_element_type=F32)


def _dot_nt(a, b):
    return lax.dot_general(a, b, (((1,), (1,)), ((), ())), preferred_element_type=F32)


def _dot_tn(a, b):
    return lax.dot_general(a, b, (((0,), (0,)), ((), ())), preferred_element_type=F32)


def _sigmoid(x):
    return 1.0 / (1.0 + jnp.exp(-x))


def _silu(x):
    return x * _sigmoid(x)


def _softplus(x):
    return jnp.maximum(x, 0.0) + jnp.log(1.0 + jnp.exp(-jnp.abs(x)))


def _rms(x, w):
    return x * lax.rsqrt(jnp.mean(x * x, axis=-1, keepdims=True) + EPS) * w


def _chunk_cumsum(x, axis, reverse):
    n = x.shape[axis]
    pos = lax.broadcasted_iota(jnp.int32, x.shape, axis) % CHUNK
    s = 1
    while s < CHUNK:
        if reverse:
            shifted = pltpu.roll(x, n - s, axis)
            x = x + jnp.where(pos < CHUNK - s, shifted, 0.0)
        else:
            shifted = pltpu.roll(x, s, axis)
            x = x + jnp.where(pos >= s, shifted, 0.0)
        s *= 2
    return x


def _decay_beta(z, bias, neg_a, axis):
    idx = lax.broadcasted_iota(jnp.int32, z.shape, 1 - axis)
    g = neg_a * _softplus(z + bias)
    g_f = _chunk_cumsum(jnp.where(idx < GDN_HEADS, g, 0.0), axis, reverse=False)
    g_b = _chunk_cumsum(jnp.where((idx >= GDN_HEADS) & (idx < 2 * GDN_HEADS), g, 0.0), axis, reverse=True)
    return jnp.where(idx < GDN_HEADS, g_f, jnp.where(idx < 2 * GDN_HEADS, g_b, _sigmoid(z)))


def _rope(t, cos, sin_lo, sin_hi):
    return (t * cos + pltpu.roll(t, HEAD_PAD - ROPE_HALF, 1) * sin_lo
            + pltpu.roll(t, ROPE_HALF, 1) * sin_hi)


def _in_proj_kernel(x_ref, pos_ref, n1_ref, wsm_ref, wgdn_ref, wgg_ref, wga_ref, wgb_ref, wgt_ref,
                    qn_ref, wuq_ref, kvn_ref, wukv_ref, vone_ref, ropec_ref, gcc_ref, gcr_ref,
                    q_out, k_out, v_out, zg_out, gg_out, sa_out, sb_out, gcol_out, grow_out):
    hb = _rms(x_ref[...], n1_ref[...]).astype(BF16)

    zs = _dot(hb, wsm_ref[...])
    c_q = zs[:, :MLA_Q_LORA]
    c_kv = zs[:, MLA_Q_LORA:MLA_Q_LORA + MLA_KV_LORA]
    k_r = zs[:, 384:512]
    z_gab = zs[:, 512:640]

    wide = [(sa_out, wga_ref, c0, _sigmoid) for c0 in range(0, D_MODEL, 512)]
    wide += [(sb_out, wgb_ref, c0, _sigmoid) for c0 in range(0, D_MODEL, 512)]
    wide += [(gg_out, wgg_ref, 0, _silu)]
    wide += [(zg_out, wgdn_ref, c0, None) for c0 in range(0, GDN_QKV, 512)]
    wide = iter(wide)

    def wide_piece():
        out_ref, w_ref, c0, act = next(wide)
        y = _dot(hb, w_ref[:, c0:c0 + 512])
        out_ref[:, c0:c0 + 512] = (y if act is None else act(y)).astype(BF16)

    wide_piece()
    ang = pos_ref[...].astype(F32) * ropec_ref[0:1, :]
    cos = jnp.cos(ang)
    wide_piece()
    sin = jnp.sin(ang)
    sin_lo = sin * ropec_ref[1:2, :]
    sin_hi = sin * ropec_ref[2:3, :]
    wide_piece()

    qf = _dot(_rms(c_q, qn_ref[...]).astype(BF16), wuq_ref[...])
    scale = math.log2(math.e) / math.sqrt(MLA_NOPE + MLA_ROPE)
    for h in range(MLA_HEADS):
        sl = slice(h * HEAD_PAD, (h + 1) * HEAD_PAD)
        q_out[:, sl] = (_rope(qf[:, sl], cos, sin_lo, sin_hi) * scale).astype(BF16)
        if h % 4 == 3:
            wide_piece()

    ckv = _rms(c_kv, kvn_ref[...]).astype(BF16)
    kv = _dot(ckv, wukv_ref[...])
    kr = _rope(k_r, cos, sin_lo, sin_hi)
    low = lax.broadcasted_iota(jnp.int32, (1, HEAD_PAD), 1) < MLA_NOPE
    for h in range(MLA_HEADS):
        sl = slice(h * HEAD_PAD, (h + 1) * HEAD_PAD)
        k_out[:, sl] = jnp.where(low, kv[:, sl], kr).astype(BF16)
        v_out[:, sl] = jnp.where(low, pltpu.roll(kv[:, sl], MLA_V, 1), vone_ref[...]).astype(BF16)
    wide_piece()

    col = _decay_beta(z_gab, gcc_ref[0:1, :], gcc_ref[1:2, :], axis=0)
    gcol_out[...] = col
    wide_piece()
    z_t = _dot_nt(wgt_ref[...], hb)
    grow_out[...] = _decay_beta(z_t, gcr_ref[:, 0:1], gcr_ref[:, 1:2], axis=1)
    wide_piece()
    assert next(wide, None) is None


def _in_proj(x2, pos2, w, tm):
    T = x2.shape[0]
    grid = (T // tm,)

    def full(a):
        return pl.BlockSpec(a.shape, lambda i: (0,) * a.ndim, pipeline_mode=pl.Buffered(1))

    def rows(width):
        return pl.BlockSpec((tm, width), lambda i: (i, 0))

    ins = [x2, pos2, w["n1"], w["w_small"], w["w_gdn"], w["w_gg"], w["w_ga"], w["w_gb"], w["w_gt"],
           w["qn"], w["w_uq"], w["kvn"], w["w_ukv"], w["v_one"], w["rope_c"], w["g_col_c"], w["g_row_c"]]
    in_specs = [rows(D_MODEL), rows(1)] + [full(a) for a in ins[2:]]
    out_shape = [
        jax.ShapeDtypeStruct((T, MLA_HEADS * HEAD_PAD), BF16),
        jax.ShapeDtypeStruct((T, MLA_HEADS * HEAD_PAD), BF16),
        jax.ShapeDtypeStruct((T, MLA_HEADS * HEAD_PAD), BF16),
        jax.ShapeDtypeStruct((T, GDN_QKV), BF16),
        jax.ShapeDtypeStruct((T, GDN_WIDTH), BF16),
        jax.ShapeDtypeStruct((T, D_MODEL), BF16),
        jax.ShapeDtypeStruct((T, D_MODEL), BF16),
        jax.ShapeDtypeStruct((T, HEAD_PAD), F32),
        jax.ShapeDtypeStruct((4 * GDN_HEADS, T), F32),
    ]
    out_specs = [rows(MLA_HEADS * HEAD_PAD), rows(MLA_HEADS * HEAD_PAD), rows(MLA_HEADS * HEAD_PAD), rows(GDN_QKV),
                 rows(GDN_WIDTH), rows(D_MODEL), rows(D_MODEL), rows(HEAD_PAD),
                 pl.BlockSpec((4 * GDN_HEADS, tm), lambda i: (0, i))]
    return pl.pallas_call(
        _in_proj_kernel, grid=grid, in_specs=in_specs, out_specs=out_specs, out_shape=out_shape,
        name="in_proj",
        compiler_params=pltpu.CompilerParams(dimension_semantics=("parallel",),
                                             vmem_limit_bytes=VMEM_LIMIT_BYTES),
    )(*ins)


ATTN_GROUP = 2
ATTN_Q_TILES = 4
GDN_HALO = 16


def _attn_kernel(q_ref, k_ref, v_ref, *rest, tq):
    z_refs, zp_refs, zn_refs, cw_refs = rest[0:3], rest[3:6], rest[6:9], rest[9:12]
    o_ref, conv_out_refs, pad_s = rest[12], rest[13:16], rest[16]
    R = q_ref.shape[0]
    it = pl.program_id(2)
    halo = GDN_HALO

    for s, (z_ref, zp_ref, zn_ref) in enumerate(zip(z_refs, zp_refs, zn_refs)):
        pad_s[s, 0:halo, :] = jnp.where(it == 0, 0.0, zp_ref[...].astype(F32))
        pad_s[s, halo:halo + R, :] = z_ref[...].astype(F32)
        pad_s[s, halo + R:, :] = jnp.where(it == pl.num_programs(2) - 1, 0.0, zn_ref[...].astype(F32))

    stream_norm = ((True, GDN_DK ** -0.5), (True, 1.0), (False, 1.0))

    for i in range(R // tq):
        rows = slice(i * tq, (i + 1) * tq)
        for s, (cw_ref, out_ref, (l2, mult)) in enumerate(zip(cw_refs, conv_out_refs, stream_norm)):
            acc = jnp.zeros((tq, GDN_DK), F32)
            for t in range(GDN_CONV):
                off = halo + i * tq + t - GDN_CONV // 2
                acc = acc + pad_s[s, off:off + tq, :] * cw_ref[t:t + 1, :]
            y = _silu(acc)
            if l2:
                y = y * (lax.rsqrt(jnp.sum(y * y, axis=-1, keepdims=True) + EPS) * mult)
            out_ref[rows, :] = y.astype(BF16)

        scores = [_dot_nt(q_ref[rows, j * HEAD_PAD:(j + 1) * HEAD_PAD], k_ref[:, j * HEAD_PAD:(j + 1) * HEAD_PAD])
                  for j in range(ATTN_GROUP)]

        outs = []
        for j, s in enumerate(scores):
            p = jnp.exp2((s - jnp.max(s, axis=-1, keepdims=True)).astype(BF16))
            o = _dot(p, v_ref[:, j * HEAD_PAD:(j + 1) * HEAD_PAD])
            outs.append(o[:, :MLA_V] / o[:, MLA_V:MLA_V + 1])
        o_ref[rows, :] = jnp.concatenate(outs, axis=-1).astype(BF16)


def _attention(q, k, v, zg, conv_w, tq):
    B, S, _ = q.shape
    H = GDN_HEADS
    assert MLA_HEADS // ATTN_GROUP == H
    R = min(ATTN_Q_TILES * tq, S)
    hb = R // GDN_HALO
    last_hb = S // GDN_HALO - 1
    kv = pl.BlockSpec((None, S, ATTN_GROUP * HEAD_PAD), lambda b, g, r: (b, 0, g))
    z_main = [pl.BlockSpec((None, R, GDN_DK), lambda b, g, r, s=s: (b, r, s * H + g)) for s in range(3)]
    z_prev = [pl.BlockSpec((None, GDN_HALO, GDN_DK), lambda b, g, r, s=s: (b, jnp.maximum(r * hb - 1, 0), s * H + g))
              for s in range(3)]
    z_next = [pl.BlockSpec((None, GDN_HALO, GDN_DK),
                           lambda b, g, r, s=s: (b, jnp.minimum((r + 1) * hb, last_hb), s * H + g)) for s in range(3)]
    cw = [pl.BlockSpec((GDN_CONV, GDN_DK), lambda b, g, r, s=s: (0, s * H + g)) for s in range(3)]
    slab_out = pl.BlockSpec((None, R, GDN_DK), lambda b, g, r: (b, r, g))
    slab_shape = jax.ShapeDtypeStruct((B, S, H * GDN_DK), BF16)
    return pl.pallas_call(
        functools.partial(_attn_kernel, tq=tq), grid=(B, MLA_HEADS // ATTN_GROUP, S // R),
        in_specs=[pl.BlockSpec((None, R, ATTN_GROUP * HEAD_PAD), lambda b, g, r: (b, r, g)), kv, kv]
                 + z_main + z_prev + z_next + cw,
        out_specs=[pl.BlockSpec((None, R, ATTN_GROUP * MLA_V), lambda b, g, r: (b, r, g)),
                   slab_out, slab_out, slab_out],
        out_shape=[jax.ShapeDtypeStruct((B, S, MLA_WIDTH), BF16), slab_shape, slab_shape, slab_shape],
        scratch_shapes=[pltpu.VMEM((3, R + 2 * GDN_HALO, GDN_DK), F32)],
        name="mla_attention",
        compiler_params=pltpu.CompilerParams(dimension_semantics=("parallel", "parallel", "parallel"),
                                             vmem_limit_bytes=VMEM_LIMIT_BYTES),
    )(q, k, v, *([zg] * 9), *([conv_w] * 3))


def _bmm(a, b):
    return jnp.einsum("kij,kjl->kil", a, b, preferred_element_type=F32)


def _bmm_nt(a, b):
    return jnp.einsum("kid,kjd->kij", a, b, preferred_element_type=F32)


def _unit_tri_solve(l_mat, rhs, on_diag, eye):
    n1 = jnp.where(on_diag, -l_mat, 0.0)
    l_off = jnp.where(on_diag, 0.0, l_mat)
    n2 = _bmm(n1, n1)
    d = eye + n1
    d = d + _bmm(d, n2)
    n4 = _bmm(n2, n2)
    d = d + _bmm(d, n4)
    n8 = _bmm(n4, n4)
    d = d + _bmm(d, n8)
    db = d.astype(BF16)
    m = _bmm(db, l_off.astype(BF16))
    y0 = _bmm(db, rhs.astype(BF16))
    mb = m.astype(BF16)
    m2 = _bmm(mb, mb)
    z1 = y0 - _bmm(mb, y0.astype(BF16))
    return z1 + _bmm(m2.astype(BF16), z1.astype(BF16))


def _gdn_prep_kernel(qc_ref, kc_ref, vc_ref, gcol_ref, grow_ref,
                     uf_ref, wf_ref, qef_ref, kdf_ref, qkf_ref,
                     ub_ref, wb_ref, qeb_ref, kdb_ref, qkb_ref, et_ref):
    seg = qc_ref.shape[0]
    K = seg // CHUNK

    row = lax.broadcasted_iota(jnp.int32, (CHUNK, CHUNK), 0)
    col = lax.broadcasted_iota(jnp.int32, (CHUNK, CHUNK), 1)
    on_diag = (row // SUB) == (col // SUB)
    eye = (row == col).astype(F32)
    gc = gcol_ref[...].reshape(K, CHUNK, HEAD_PAD)

    systems, rhs_all, stores = [], [], []
    for h in range(GDN_HEADS):
        hs = slice(h * GDN_DK, (h + 1) * GDN_DK)
        qb = qc_ref[:, hs].reshape(K, CHUNK, GDN_DK)
        kb = kc_ref[:, hs].reshape(K, CHUNK, GDN_DK)
        q3, k3, v3 = qb.astype(F32), kb.astype(F32), vc_ref[:, hs].astype(F32).reshape(K, CHUNK, GDN_DV)
        gram = _bmm_nt(kb, kb)
        qk = _bmm_nt(qb, kb)
        for d, keep_incl, keep_strict, total_row, refs in (
                (0, row >= col, row > col, CHUNK - 1, (uf_ref, wf_ref, qef_ref, kdf_ref, qkf_ref)),
                (1, row <= col, row < col, 0, (ub_ref, wb_ref, qeb_ref, kdb_ref, qkb_ref))):
            u_ref, w_ref, qe_ref, kd_ref, qk_ref = refs
            lane = d * GDN_HEADS + h
            g_col = gc[:, :, lane:lane + 1]
            g_row = grow_ref[:, lane:lane + 1, :]
            beta = gc[:, :, 2 * GDN_HEADS + lane:2 * GDN_HEADS + lane + 1]
            decay = jnp.where(keep_incl, jnp.exp(jnp.where(keep_incl, g_col - g_row, 0.0)), 0.0)
            e_g = jnp.exp(g_col)
            g_total = g_col[:, total_row:total_row + 1, :]
            systems.append(jnp.where(keep_strict, gram * beta * decay, 0.0))
            rhs_all.append(jnp.concatenate([v3 * beta, k3 * (beta * e_g)], axis=-1))
            stores.append((u_ref, w_ref, hs))
            qe_ref[:, hs] = (q3 * e_g).astype(BF16).reshape(seg, GDN_DK)
            kd_ref[:, hs] = (k3 * jnp.exp(g_total - g_col)).astype(BF16).reshape(seg, GDN_DK)
            qk_ref[h] = (qk * decay).astype(BF16).reshape(seg, CHUNK)
            et_ref[:, lane:lane + 1, :] = jnp.broadcast_to(jnp.exp(g_total), (K, 1, GDN_DV))

    sol = _unit_tri_solve(jnp.concatenate(systems, axis=0), jnp.concatenate(rhs_all, axis=0), on_diag, eye)
    for i, (u_ref, w_ref, hs) in enumerate(stores):
        part = sol[i * K:(i + 1) * K]
        u_ref[:, hs] = part[:, :, :GDN_DV].astype(BF16).reshape(seg, GDN_DV)
        w_ref[:, hs] = part[:, :, GDN_DV:].astype(BF16).reshape(seg, GDN_DK)


def _gdn_scan_kernel(uf_ref, wf_ref, qef_ref, kdf_ref, qkf_ref, etf_ref,
                     ub_ref, wb_ref, qeb_ref, kdb_ref, qkb_ref, etb_ref,
                     of_ref, ob_ref, st_ref):
    seg = uf_ref.shape[0]
    K = seg // CHUNK

    @pl.when(pl.program_id(1) == 0)
    def _():
        st_ref[...] = jnp.zeros(st_ref.shape, F32)

    dirs = ((uf_ref, wf_ref, qef_ref, kdf_ref, qkf_ref, etf_ref, of_ref),
            (ub_ref, wb_ref, qeb_ref, kdb_ref, qkb_ref, etb_ref, ob_ref))

    def step(t, carry):
        chains = []
        for d, (u_ref, w_ref, qe_ref, kd_ref, qk_ref, et_ref, o_ref) in enumerate(dirs):
            c = t if d == 0 else K - 1 - t
            rows = pl.ds(pl.multiple_of(c * CHUNK, CHUNK), CHUNK)
            for h in range(GDN_HEADS):
                chains.append((d * GDN_HEADS + h, h, c, rows, slice(h * GDN_DK, (h + 1) * GDN_DK),
                               u_ref, w_ref, qe_ref, kd_ref, qk_ref, et_ref, o_ref))
        sb = [st_ref[ch[0]].astype(BF16) for ch in chains]
        v_new = [u_ref[rows, hs] - _dot(w_ref[rows, hs], s)
                 for s, (_, _, _, rows, hs, u_ref, w_ref, *_) in zip(sb, chains)]
        o_inter = [_dot(qe_ref[rows, hs], s)
                   for s, (_, _, _, rows, hs, _, _, qe_ref, *_) in zip(sb, chains)]
        vb = [v.astype(BF16) for v in v_new]
        for v, oi, (lane, h, c, rows, hs, _, _, _, kd_ref, qk_ref, et_ref, o_ref) in zip(vb, o_inter, chains):
            o_ref[rows, hs] = (oi + _dot(qk_ref[h, rows, :], v)).astype(BF16)
            st_ref[lane] = st_ref[lane] * et_ref[c, lane:lane + 1, :] + _dot_tn(kd_ref[rows, hs], v)
        return carry

    lax.fori_loop(0, K, step, 0, unroll=True)


def _gdn(qc, kc, vc, gcol, grow, prep_seg, scan_seg):
    B, S, _ = qc.shape
    H = GDN_HEADS

    wide = lambda dt: jax.ShapeDtypeStruct((B, S, H * GDN_DK), dt)
    qk_shape = jax.ShapeDtypeStruct((B, H, S, CHUNK), BF16)
    et_shape = jax.ShapeDtypeStruct((B, S // CHUNK, 2 * H, GDN_DV), F32)
    dir_shapes = [wide(BF16), wide(BF16), wide(BF16), wide(BF16), qk_shape]
    here3 = lambda b, j: (b, j, 0)
    here4 = lambda b, j: (b, 0, j, 0)

    seg, K = prep_seg, prep_seg // CHUNK
    seg_spec = lambda imap: pl.BlockSpec((None, seg, H * GDN_DK), imap)
    qk_spec = lambda imap: pl.BlockSpec((None, H, seg, CHUNK), imap)
    et_spec = lambda imap: pl.BlockSpec((None, K, 2 * H, GDN_DV), imap)
    dir_specs = [seg_spec(here3)] * 4 + [qk_spec(here4)]
    prep = pl.pallas_call(
        _gdn_prep_kernel, grid=(B, S // seg),
        in_specs=[seg_spec(here3), seg_spec(here3), seg_spec(here3),
                  pl.BlockSpec((None, seg, HEAD_PAD), here3),
                  pl.BlockSpec((None, K, 4 * H, CHUNK), lambda b, j: (b, j, 0, 0))],
        out_specs=dir_specs + dir_specs + [et_spec(lambda b, j: (b, j, 0, 0))],
        out_shape=dir_shapes + dir_shapes + [et_shape],
        name="gdn_prep",
        compiler_params=pltpu.CompilerParams(dimension_semantics=("parallel", "parallel"),
                                             vmem_limit_bytes=VMEM_LIMIT_BYTES),
    )(qc, kc, vc, gcol, grow)
    fwd, bwd, et = prep[:5], prep[5:10], prep[10]

    seg, K = scan_seg, scan_seg // CHUNK
    nseg = S // seg
    seg_spec = lambda imap: pl.BlockSpec((None, seg, H * GDN_DK), imap)
    qk_spec = lambda imap: pl.BlockSpec((None, H, seg, CHUNK), imap)
    et_spec = lambda imap: pl.BlockSpec((None, K, 2 * H, GDN_DV), imap)
    back3 = lambda b, j: (b, nseg - 1 - j, 0)
    back4 = lambda b, j: (b, 0, nseg - 1 - j, 0)
    return pl.pallas_call(
        _gdn_scan_kernel, grid=(B, nseg),
        in_specs=[seg_spec(here3)] * 4 + [qk_spec(here4), et_spec(lambda b, j: (b, j, 0, 0))]
                 + [seg_spec(back3)] * 4 + [qk_spec(back4), et_spec(lambda b, j: (b, nseg - 1 - j, 0, 0))],
        out_specs=[seg_spec(here3), seg_spec(back3)],
        out_shape=[wide(BF16), wide(BF16)],
        scratch_shapes=[pltpu.VMEM((2 * H, GDN_DK, GDN_DV), F32)],
        name="gdn_scan",
        compiler_params=pltpu.CompilerParams(dimension_semantics=("parallel", "arbitrary"),
                                             vmem_limit_bytes=VMEM_LIMIT_BYTES),
    )(*fwd, et, *bwd, et)


def _out_kernel(x_ref, oa_ref, of_ref, ob_ref, gg_ref, sa_ref, sb_ref, gnw_ref, wpa_ref, wpb_ref, wo_ref,
                n2_ref, w1_ref, w2_ref, nf_ref, y_ref):
    pa = _dot(oa_ref[...], wpa_ref[...])

    parts = []
    for h in range(GDN_HEADS):
        hs = slice(h * GDN_DV, (h + 1) * GDN_DV)
        o = of_ref[:, hs].astype(F32) + ob_ref[:, hs].astype(F32)
        parts.append((_rms(o, gnw_ref[...]) * gg_ref[:, hs].astype(F32)).astype(BF16))
    o_b = jnp.concatenate(parts, axis=-1)

    merged = sa_ref[...].astype(F32) * pa + sb_ref[...].astype(F32) * _dot(o_b, wpb_ref[...])
    x1 = x_ref[...] + _dot(merged.astype(BF16), wo_ref[...])
    h2 = _rms(x1, n2_ref[...]).astype(BF16)
    a = jnp.maximum(_dot(h2, w1_ref[...]), 0.0)
    x2 = x1 + _dot((a * a).astype(BF16), w2_ref[...])
    y_ref[...] = _rms(x2, nf_ref[...])


def _out_proj(x2, oa, o_f, o_b, gg, sa, sb, w, tm):
    T = x2.shape[0]

    def full(a):
        return pl.BlockSpec(a.shape, lambda i: (0,) * a.ndim, pipeline_mode=pl.Buffered(1))

    def rows(width):
        return pl.BlockSpec((tm, width), lambda i: (i, 0))

    consts = [w["gnw"], w["w_pa"], w["w_pb"], w["w_o"], w["n2"], w["w_1"], w["w_2"], w["nf"]]
    return pl.pallas_call(
        _out_kernel, grid=(T // tm,),
        in_specs=[rows(D_MODEL), rows(MLA_WIDTH), rows(GDN_WIDTH), rows(GDN_WIDTH), rows(GDN_WIDTH),
                  rows(D_MODEL), rows(D_MODEL)] + [full(a) for a in consts],
        out_specs=rows(D_MODEL),
        out_shape=jax.ShapeDtypeStruct((T, D_MODEL), F32),
        name="merge_mlp",
        compiler_params=pltpu.CompilerParams(dimension_semantics=("parallel",),
                                             vmem_limit_bytes=VMEM_LIMIT_BYTES),
    )(x2, oa, o_f, o_b, gg, sa, sb, *consts)


def _prepare_weights(norm1_w, w_in, q_norm_w, w_uq, kv_norm_w, w_ukv, a_log_f, dt_bias_f, a_log_b,
                     dt_bias_b, gdn_norm_w, w_proj_a, w_proj_b, w_out, norm2_w, w_ff1, w_ff2, final_norm_w):
    H = GDN_HEADS
    o_kv = MLA_Q_LORA
    o_kr = o_kv + MLA_KV_LORA
    o_gdn = o_kr + MLA_ROPE
    o_gab = o_gdn + GDN_QKV
    o_gg = o_gab + 4 * H
    o_ga = o_gg + GDN_WIDTH
    o_gb = o_ga + D_MODEL
    zeros = lambda n: jnp.zeros((D_MODEL, n), F32)
    w_kr = jnp.concatenate([zeros(MLA_NOPE), w_in[:, o_kr:o_gdn], zeros(HEAD_PAD - MLA_NOPE - MLA_ROPE)], 1)
    w_gab = w_in[:, o_gab:o_gg]
    w_small = jnp.concatenate([w_in[:, :o_kr], w_kr, w_gab, zeros(HEAD_PAD - 4 * H)], 1)

    dq = MLA_NOPE + MLA_ROPE
    w_uq_p = jnp.pad(w_uq.reshape(MLA_Q_LORA, MLA_HEADS, dq), ((0, 0), (0, 0), (0, HEAD_PAD - dq)))
    assert MLA_NOPE + MLA_V == HEAD_PAD and MLA_NOPE == MLA_V
    v_one = (jnp.arange(HEAD_PAD) == MLA_V).astype(F32).reshape(1, HEAD_PAD)

    inv_freq = 1.0 / (ROPE_THETA ** (jnp.arange(0, MLA_ROPE, 2, dtype=F32) / MLA_ROPE))
    z16 = jnp.zeros((ROPE_HALF,), F32)
    one16 = jnp.ones((ROPE_HALF,), F32)
    lane = lambda lo, hi: jnp.concatenate([jnp.zeros((MLA_NOPE,), F32), lo, hi,
                                           jnp.zeros((HEAD_PAD - MLA_NOPE - MLA_ROPE,), F32)])
    rope_c = jnp.stack([lane(inv_freq, inv_freq), lane(-one16, z16), lane(z16, one16)])

    bias = jnp.concatenate([dt_bias_f, dt_bias_b, jnp.zeros((2 * H,), F32)])
    neg_a = jnp.concatenate([-jnp.exp(a_log_f), -jnp.exp(a_log_b), jnp.zeros((2 * H,), F32)])
    g_row_c = jnp.stack([bias, neg_a], axis=1)
    g_col_c = jnp.pad(jnp.stack([bias, neg_a]), ((0, 0), (0, HEAD_PAD - 4 * H)))

    return {
        "n1": norm1_w.reshape(1, D_MODEL),
        "w_small": w_small.astype(BF16),
        "w_gdn": w_in[:, o_gdn:o_gab].astype(BF16),
        "w_gg": w_in[:, o_gg:o_ga].astype(BF16),
        "w_ga": w_in[:, o_ga:o_gb].astype(BF16),
        "w_gb": w_in[:, o_gb:].astype(BF16),
        "w_gt": w_gab.T.astype(BF16),
        "qn": q_norm_w.reshape(1, MLA_Q_LORA),
        "w_uq": w_uq_p.reshape(MLA_Q_LORA, MLA_HEADS * HEAD_PAD).astype(BF16),
        "kvn": kv_norm_w.reshape(1, MLA_KV_LORA),
        "w_ukv": w_ukv.astype(BF16),
        "v_one": v_one,
        "rope_c": rope_c,
        "g_col_c": g_col_c,
        "g_row_c": g_row_c,
        "gnw": gdn_norm_w.reshape(1, GDN_DV),
        "w_pa": w_proj_a.astype(BF16),
        "w_pb": w_proj_b.astype(BF16),
        "w_o": w_out.astype(BF16),
        "n2": norm2_w.reshape(1, D_MODEL),
        "w_1": w_ff1.astype(BF16),
        "w_2": w_ff2.astype(BF16),
        "nf": final_norm_w.reshape(1, D_MODEL),
    }


def _tile_sizes(S):
    return dict(
        tm=min(512, S),
        tq=min(512, S),
        prep_seg=min(512, S),
        scan_seg=min(1024, S),
    )


def _layer(x, positions, conv_w, w):
    B, S, _ = x.shape
    T = B * S
    ts = _tile_sizes(S)
    tm = ts["tm"]
    x2 = x.reshape(T, D_MODEL)
    q, k, v, zg, gg, sa, sb, gcol, grow = _in_proj(x2, positions.reshape(T, 1), w, tm)

    o_a, qc, kc, vc = _attention(q.reshape(B, S, -1), k.reshape(B, S, -1), v.reshape(B, S, -1),
                                 zg.reshape(B, S, GDN_QKV), conv_w, ts["tq"])

    grow_c = grow.reshape(4 * GDN_HEADS, B, S // CHUNK, CHUNK).transpose(1, 2, 0, 3)
    o_f, o_b = _gdn(qc, kc, vc, gcol.reshape(B, S, HEAD_PAD), grow_c, ts["prep_seg"], ts["scan_seg"])

    y = _out_proj(x2, o_a.reshape(T, MLA_WIDTH), o_f.reshape(T, GDN_WIDTH), o_b.reshape(T, GDN_WIDTH),
                  gg, sa, sb, w, tm)
    return y.reshape(B, S, D_MODEL)


def kernel(x, positions, norm1_w, w_in, q_norm_w, w_uq, kv_norm_w, w_ukv, conv_w, a_log_f, dt_bias_f,
           a_log_b, dt_bias_b, gdn_norm_w, w_proj_a, w_proj_b, w_out, norm2_w, w_ff1, w_ff2,
           final_norm_w):
    assert norm1_w.shape[0] == 1, "single-layer block"
    w = _prepare_weights(norm1_w[0], w_in[0], q_norm_w[0], w_uq[0], kv_norm_w[0], w_ukv[0], a_log_f[0],
                         dt_bias_f[0], a_log_b[0], dt_bias_b[0], gdn_norm_w[0], w_proj_a[0], w_proj_b[0],
                         w_out[0], norm2_w[0], w_ff1[0], w_ff2[0], final_norm_w)
    return _layer(x, positions, conv_w[0], w)
```
